```python
import math
import jax, jax.numpy as jnp
from jax import lax
import numpy as np

D_MODEL = 4096
BATCH = 4
SEQ = 4096
DEPTH = 2

GRID_W = 64
CTX_LEN = 256
HEAD_DIM = 128
BRANCH_WIDTH = D_MODEL // 4
N_BRANCHES = 3
NA_HEADS = BRANCH_WIDTH // HEAD_DIM
NA_WIN_ROWS = 8
NA_WIN_COLS = 16
NA_QCOL_BLK = 16
NA_KCOL_BLK = NA_QCOL_BLK + NA_WIN_COLS
SW_HEADS = BRANCH_WIDTH // HEAD_DIM
SW_KV_HEADS = SW_HEADS // 4
SW_WINDOW = 128
SW_BLOCK = 128
DF_HEADS = BRANCH_WIDTH // (2 * HEAD_DIM)
DF_V_DIM = 2 * HEAD_DIM
DF_Q_BLOCK = 128
D_FF = 7 * D_MODEL // 4
N_MOD = 9
ROPE_BASE = 10000.0
EPS = 1e-6
NEG_INF = -1e30
SPLIT_SIZES = (NA_HEADS * HEAD_DIM, NA_HEADS * HEAD_DIM, NA_HEADS * HEAD_DIM,
               SW_HEADS * HEAD_DIM, SW_KV_HEADS * HEAD_DIM, SW_KV_HEADS * HEAD_DIM,
               DF_HEADS * 2 * HEAD_DIM, DF_HEADS * 2 * HEAD_DIM, DF_HEADS * DF_V_DIM)
IN_WIDTH = sum(SPLIT_SIZES)

kernel_name = "hybrid_natten_swa_diffattn_macaron_dit"


def rms_norm(x, g):
    x32 = x.astype(jnp.float32)
    y = x32 * lax.rsqrt(jnp.mean(x32 * x32, axis=-1, keepdims=True) + EPS)
    return (y * g.astype(jnp.float32)).astype(x.dtype)


def modulate(x, g, shift, scale):
    return rms_norm(x, g) * (1.0 + scale) + shift


def adaln(cvec, w, b):
    m = jax.nn.silu(cvec) @ w + b
    return jnp.split(m[..., None, :], N_MOD, axis=-1)


def swiglu(h, w_g, w_u, w_d):
    return (jax.nn.silu(h @ w_g) * (h @ w_u)) @ w_d


def axial_rope_tables(n):
    t = jnp.arange(n, dtype=jnp.int32)
    row = (t // GRID_W).astype(jnp.float32)
    col = (t % GRID_W).astype(jnp.float32)
    half = HEAD_DIM // 2
    inv = ROPE_BASE ** (-jnp.arange(0, half, 2, dtype=jnp.float32) / half)
    ang = jnp.concatenate([row[:, None] * inv, col[:, None] * inv], axis=-1)
    return jnp.cos(ang), jnp.sin(ang)


def apply_rope(x, cos, sin):
    qd = HEAD_DIM // 4
    shp = (x.shape[1],) + (1,) * (x.ndim - 3) + (qd,)
    out = []
    for a in range(2):
        seg = x[..., 2 * a * qd:2 * (a + 1) * qd]
        x1, x2 = seg[..., :qd], seg[..., qd:]
        cs = cos[:, a * qd:(a + 1) * qd].reshape(shp).astype(x.dtype)
        sn = sin[:, a * qd:(a + 1) * qd].reshape(shp).astype(x.dtype)
        out += [x1 * cs - x2 * sn, x2 * cs + x1 * sn]
    return jnp.concatenate(out, axis=-1)


def neighbourhood_attention(q, k, v, k_ctx, v_ctx, rpb):
    b, n, h, dh = q.shape
    rows = n // GRID_W
    kh = min(NA_WIN_ROWS, rows)
    kw = NA_WIN_COLS
    ncb = GRID_W // NA_QCOL_BLK
    scale = dh ** -0.5
    qcol = np.arange(GRID_W).reshape(ncb, NA_QCOL_BLK)
    col_start = np.clip(qcol - kw // 2, 0, GRID_W - kw)
    blk_start = np.clip(np.arange(ncb) * NA_QCOL_BLK - kw // 2, 0, GRID_W - NA_KCOL_BLK)
    kcol = blk_start[:, None] + np.arange(NA_KCOL_BLK)
    col_ok = (kcol[:, None, :] >= col_start[..., None]) & (kcol[:, None, :] < col_start[..., None] + kw)
    bias_col = kcol[:, None, :] - qcol[:, :, None] + NA_WIN_COLS - 1
    row_start = np.clip(np.arange(rows) - kh // 2, 0, rows - kh)
    bias_row = row_start[:, None] + np.arange(kh)[None, :] - np.arange(rows)[:, None] + NA_WIN_ROWS - 1
    qg = q.reshape(b, rows, ncb, NA_QCOL_BLK, h, dh).transpose(1, 0, 2, 3, 4, 5)
    kg = k.reshape(b, rows, GRID_W, h, dh)
    vg = v.reshape(b, rows, GRID_W, h, dh)
    nloc = kh * NA_KCOL_BLK

    def row_fn(args):
        q_row, r0, dr_idx = args
        k_blk = lax.dynamic_slice_in_dim(kg, r0, kh, axis=1)[:, :, kcol]
        v_blk = lax.dynamic_slice_in_dim(vg, r0, kh, axis=1)[:, :, kcol]
        s_loc = jnp.einsum('bjqhd,brjmhd->bhjqrm', q_row, k_blk).astype(jnp.float32) * scale
        bias = rpb[:, dr_idx][:, :, bias_col].transpose(0, 2, 3, 1, 4).astype(jnp.float32)
        s_loc = jnp.where(col_ok[None, None, :, :, None, :], s_loc + bias[None], NEG_INF)
        s_ctx = jnp.einsum('bjqhd,blhd->bhjql', q_row, k_ctx).astype(jnp.float32) * scale
        s = jnp.concatenate([s_loc.reshape(s_loc.shape[:4] + (nloc,)), s_ctx], axis=-1)
        p = jax.nn.softmax(s, axis=-1).astype(v.dtype)
        p_loc = p[..., :nloc].reshape(s_loc.shape)
        return (jnp.einsum('bhjqrm,brjmhd->bjqhd', p_loc, v_blk)
                + jnp.einsum('bhjql,blhd->bjqhd', p[..., nloc:], v_ctx))

    out = lax.map(row_fn, (qg, jnp.asarray(row_start, jnp.int32), jnp.asarray(bias_row, jnp.int32)))
    return out.transpose(1, 0, 2, 3, 4, 5).reshape(b, n, h * dh)


def sliding_window_gqa(q, k, v, k_ctx, v_ctx, sink):
    b, n, hq, dh = q.shape
    hkv = k.shape[2]
    g = hq // hkv
    nb = n // SW_BLOCK
    scale = dh ** -0.5
    qb = q.reshape(b, nb, SW_BLOCK, hkv, g, dh)
    pad = ((0, 0), (SW_BLOCK, SW_BLOCK), (0, 0), (0, 0))
    kp = jnp.pad(k, pad).reshape(b, nb + 2, SW_BLOCK, hkv, dh)
    vp = jnp.pad(v, pad).reshape(b, nb + 2, SW_BLOCK, hkv, dh)
    kb = jnp.concatenate([kp[:, :-2], kp[:, 1:-1], kp[:, 2:]], axis=2)
    vb = jnp.concatenate([vp[:, :-2], vp[:, 1:-1], vp[:, 2:]], axis=2)
    qpos = jnp.arange(n).reshape(nb, SW_BLOCK)
    kpos = (jnp.arange(nb)[:, None] - 1) * SW_BLOCK + jnp.arange(3 * SW_BLOCK)[None, :]
    ok = ((kpos[:, None, :] >= 0) & (kpos[:, None, :] < n)
          & (jnp.abs(qpos[:, :, None] - kpos[:, None, :]) <= SW_WINDOW))
    s_loc = jnp.einsum('bnqkgd,bnmkd->bkgnqm', qb, kb).astype(jnp.float32) * scale
    s_loc = jnp.where(ok[None, None, None], s_loc, NEG_INF)
    s_ctx = jnp.einsum('bnqkgd,blkd->bkgnql', qb, k_ctx).astype(jnp.float32) * scale
    s_sink = jnp.broadcast_to(sink.reshape(hkv, g)[None, :, :, None, None, None].astype(jnp.float32),
                              s_loc.shape[:-1] + (1,))
    p = jax.nn.softmax(jnp.concatenate([s_loc, s_ctx, s_sink], axis=-1), axis=-1).astype(v.dtype)
    m = 3 * SW_BLOCK
    l = k_ctx.shape[1]
    o = (jnp.einsum('bkgnqm,bnmkd->bnqkgd', p[..., :m], vb)
         + jnp.einsum('bkgnql,blkd->bnqkgd', p[..., m:m + l], v_ctx))
    return o.reshape(b, n, hq * dh)


def context_attention(q, k, v, sink=None):
    b, l, hq, dh = q.shape
    hkv = k.shape[2]
    g = hq // hkv
    qg = q.reshape(b, l, hkv, g, dh)
    s = jnp.einsum('bqkgd,blkd->bkgql', qg, k).astype(jnp.float32) * dh ** -0.5
    if sink is not None:
        s_sink = jnp.broadcast_to(sink.reshape(hkv, g)[None, :, :, None, None].astype(jnp.float32),
                                  s.shape[:-1] + (1,))
        s = jnp.concatenate([s, s_sink], axis=-1)
    p = jax.nn.softmax(s, axis=-1)[..., :l].astype(v.dtype)
    return jnp.einsum('bkgql,blkd->bqkgd', p, v).reshape(b, l, hq * dh)


def differential_attention(q, k, v, lam, subln_g, lambda_init):
    b, nq, h, _, dh = q.shape
    ve = v.shape[-1]
    scale = dh ** -0.5
    qb = q.reshape(b, nq // DF_Q_BLOCK, DF_Q_BLOCK, h, 2, dh).transpose(1, 0, 2, 3, 4, 5)

    def block(qi):
        s = jnp.einsum('bqhcd,bkhcd->bhcqk', qi, k).astype(jnp.float32) * scale
        p = jax.nn.softmax(s, axis=-1)
        a = (p[:, :, 0] - lam * p[:, :, 1]).astype(v.dtype)
        return jnp.einsum('bhqk,bkhe->bqhe', a, v)

    o = lax.map(block, qb).transpose(1, 0, 2, 3, 4).reshape(b, nq, h, ve)
    o = rms_norm(o, subln_g) * (1.0 - lambda_init)
    return o.reshape(b, nq, h * ve)


def merge_branches(h, outs, w_branch, w_gate, b_gate, w_out):
    merged = jnp.zeros_like(h)
    for i, o in enumerate(outs):
        merged = merged + jax.nn.sigmoid(h @ w_gate[i] + b_gate[i]) * (o @ w_branch[i])
    return merged @ w_out


def heads(t, *shape):
    return t.reshape(t.shape[:2] + shape)


def hybrid_mixer(h, h_ctx, cos, sin, w_in, rpb, sink, lam, subln_g, lambda_init,
                 w_branch, w_gate, b_gate, w_out, with_ctx):
    offs = np.cumsum(SPLIT_SIZES)[:-1].tolist()
    qa, ka, va, qb, kb, vb, qc, kc, vc = jnp.split(h @ w_in, offs, axis=-1)
    qa_x, ka_x, va_x, qb_x, kb_x, vb_x, qc_x, kc_x, vc_x = jnp.split(h_ctx @ w_in, offs, axis=-1)
    ka_x, va_x = heads(ka_x, NA_HEADS, HEAD_DIM), heads(va_x, NA_HEADS, HEAD_DIM)
    kb_x, vb_x = heads(kb_x, SW_KV_HEADS, HEAD_DIM), heads(vb_x, SW_KV_HEADS, HEAD_DIM)
    kc_x, vc_x = heads(kc_x, DF_HEADS, 2, HEAD_DIM), heads(vc_x, DF_HEADS, DF_V_DIM)
    o_a = neighbourhood_attention(heads(qa, NA_HEADS, HEAD_DIM), heads(ka, NA_HEADS, HEAD_DIM),
                                  heads(va, NA_HEADS, HEAD_DIM), ka_x, va_x, rpb)
    o_b = sliding_window_gqa(apply_rope(heads(qb, SW_HEADS, HEAD_DIM), cos, sin),
                             apply_rope(heads(kb, SW_KV_HEADS, HEAD_DIM), cos, sin),
                             heads(vb, SW_KV_HEADS, HEAD_DIM), kb_x, vb_x, sink)
    kc_all = jnp.concatenate([apply_rope(heads(kc, DF_HEADS, 2, HEAD_DIM), cos, sin), kc_x], axis=1)
    vc_all = jnp.concatenate([heads(vc, DF_HEADS, DF_V_DIM), vc_x], axis=1)
    o_c = differential_attention(apply_rope(heads(qc, DF_HEADS, 2, HEAD_DIM), cos, sin),
                                 kc_all, vc_all, lam, subln_g, lambda_init)
    y = merge_branches(h, (o_a, o_b, o_c), w_branch, w_gate, b_gate, w_out)
    if not with_ctx:
        return y, None
    oa_x = context_attention(heads(qa_x, NA_HEADS, HEAD_DIM), ka_x, va_x)
    ob_x = context_attention(heads(qb_x, SW_HEADS, HEAD_DIM), kb_x, vb_x, sink)
    oc_x = differential_attention(heads(qc_x, DF_HEADS, 2, HEAD_DIM), kc_x, vc_x, lam, subln_g, lambda_init)
    y_x = merge_branches(h_ctx, (oa_x, ob_x, oc_x), w_branch, w_gate, b_gate, w_out)
    return y, y_x


def setup_inputs(seed: int = 0) -> dict:
    key = jax.random.key(seed)
    ks = jax.random.split(key, 19)

    def nrm(k, shape, s):
        return jax.random.normal(k, shape, jnp.float32) * s

    return {
        "x": nrm(ks[0], (BATCH, SEQ, D_MODEL), 1.0),
        "c": nrm(ks[1], (BATCH, D_MODEL), 1.0),
        "ctx": nrm(ks[2], (BATCH, CTX_LEN, D_MODEL), 1.0),
        "c_ctx": nrm(ks[3], (D_MODEL,), 1.0),
        "w_ada": nrm(ks[4], (DEPTH, D_MODEL, N_MOD * D_MODEL), 0.3 * D_MODEL ** -0.5),
        "b_ada": nrm(ks[5], (DEPTH, N_MOD * D_MODEL), 0.02),
        "norm_g": 1.0 + nrm(ks[6], (DEPTH, 6, D_MODEL), 0.02),
        "w_ffn_gate": nrm(ks[7], (DEPTH, 2, D_MODEL, D_FF), D_MODEL ** -0.5),
        "w_ffn_up": nrm(ks[8], (DEPTH, 2, D_MODEL, D_FF), D_MODEL ** -0.5),
        "w_ffn_down": nrm(ks[9], (DEPTH, 2, D_FF, D_MODEL), D_FF ** -0.5),
        "w_in": nrm(ks[10], (DEPTH, D_MODEL, IN_WIDTH), D_MODEL ** -0.5),
        "na_rpb": nrm(ks[11], (DEPTH, NA_HEADS, 2 * NA_WIN_ROWS - 1, 2 * NA_WIN_COLS - 1), 0.1),
        "sw_sink": nrm(ks[12], (DEPTH, SW_HEADS), 1.0),
        "df_lambda": nrm(ks[13], (DEPTH, 4, HEAD_DIM), 0.1),
        "df_subln_g": 1.0 + nrm(ks[14], (DEPTH, DF_V_DIM), 0.02),
        "w_branch": nrm(ks[15], (DEPTH, N_BRANCHES, BRANCH_WIDTH, D_MODEL), BRANCH_WIDTH ** -0.5),
        "w_gate": nrm(ks[16], (DEPTH, N_BRANCHES, D_MODEL, D_MODEL), D_MODEL ** -0.5),
        "b_gate": nrm(ks[17], (DEPTH, N_BRANCHES, D_MODEL), 0.02),
        "w_out": nrm(ks[18], (DEPTH, D_MODEL, D_MODEL), D_MODEL ** -0.5),
    }


def reference(x, c, ctx, c_ctx, w_ada, b_ada, norm_g, w_ffn_gate, w_ffn_up, w_ffn_down,
              w_in, na_rpb, sw_sink, df_lambda, df_subln_g, w_branch, w_gate, b_gate, w_out):
    n = x.shape[1]
    cos, sin = axial_rope_tables(n)
    for l in range(DEPTH):
        last = l == DEPTH - 1
        lambda_init = 0.8 - 0.6 * math.exp(-0.3 * l)
        ml = adaln(c, w_ada[l], b_ada[l])
        mc = adaln(c_ctx, w_ada[l], b_ada[l])
        g = norm_g[l]
        x = x + 0.5 * ml[2] * rms_norm(swiglu(modulate(x, g[0], ml[0], ml[1]),
                                              w_ffn_gate[l, 0], w_ffn_up[l, 0], w_ffn_down[l, 0]), g[1])
        ctx = ctx + 0.5 * mc[2] * rms_norm(swiglu(modulate(ctx, g[0], mc[0], mc[1]),
                                                  w_ffn_gate[l, 0], w_ffn_up[l, 0], w_ffn_down[l, 0]), g[1])
        lv = df_lambda[l].astype(jnp.float32)
        lam = jnp.exp(jnp.sum(lv[0] * lv[1])) - jnp.exp(jnp.sum(lv[2] * lv[3])) + lambda_init
        y, y_x = hybrid_mixer(modulate(x, g[2], ml[3], ml[4]), modulate(ctx, g[2], mc[3], mc[4]),
                              cos, sin, w_in[l], na_rpb[l], sw_sink[l], lam, df_subln_g[l], lambda_init,
                              w_branch[l], w_gate[l], b_gate[l], w_out[l], not last)
        x = x + ml[5] * rms_norm(y, g[3])
        if not last:
            ctx = ctx + mc[5] * rms_norm(y_x, g[3])
        x = x + 0.5 * ml[8] * rms_norm(swiglu(modulate(x, g[4], ml[6], ml[7]),
                                              w_ffn_gate[l, 1], w_ffn_up[l, 1], w_ffn_down[l, 1]), g[5])
        if not last:
            ctx = ctx + 0.5 * mc[8] * rms_norm(swiglu(modulate(ctx, g[4], mc[6], mc[7]),
                                                      w_ffn_gate[l, 1], w_ffn_up[l, 1], w_ffn_down[l, 1]), g[5])
    return x
```

```python
import functools
import math

import numpy as np
import jax
import jax.numpy as jnp
from jax import lax
from jax.experimental import pallas as pl
from jax.experimental.pallas import tpu as pltpu

F32 = jnp.float32
BF16 = jnp.bfloat16

GRID_W = 64
HEAD_DIM = 128
NA_HEADS = 8
NA_WIN_ROWS = 8
NA_WIN_COLS = 16
SW_HEADS = 8
SW_KV_HEADS = 2
SW_GROUP = SW_HEADS // SW_KV_HEADS
SW_WINDOW = 128
SW_BLOCK = 128
DF_HEADS = 4
DF_Q_BLOCK = 128
N_MOD = 9
ROPE_BASE = 10000.0
EPS = 1e-6
NEG_INF = -1e30
ATT_SCALE = HEAD_DIM ** -0.5

OFF_QA, OFF_KA, OFF_VA = 0, 1024, 2048
OFF_QB, OFF_KB, OFF_VB = 3072, 4096, 4352
OFF_QC, OFF_KC, OFF_VC = 4608, 5632, 6656
IN_WIDTH = 7680
ROPE_RANGES = ((OFF_QB, OFF_VB), (OFF_QC, OFF_VC))
OFF_OA, OFF_OB, OFF_OC = 0, 1024, 2048
ATT_WIDTH = 3072

MOD_ROWS = 8
V7X_VMEM_LIMIT = 58 * 1024 * 1024

TM_UP, TN_UP = 1024, 512
TM_DOWN, TK_DOWN = 512, 512
TM_PROJ, TN_PROJ = 1024, 256
TM_MERGE, TN_MERGE = 1024, 256
TN_ADA = 512
EPI_ROWS = 16


def _params(sem):
    return pltpu.CompilerParams(dimension_semantics=sem, vmem_limit_bytes=V7X_VMEM_LIMIT)


def _dot(a, b):
    return jnp.dot(a, b, preferred_element_type=F32)


def _dot_t(a, b):
    return lax.dot_general(a, b, (((1,), (1,)), ((), ())), preferred_element_type=F32)


def _rstd(y):
    return lax.rsqrt(jnp.mean(y * y, axis=-1, keepdims=True) + EPS)


def _silu(a):
    return a * jax.nn.sigmoid(a)


def _adaln_body(c_ref, w_ref, b_ref, o_ref):
    s = _silu(c_ref[...]).astype(BF16)
    o_ref[...] = _dot(s, w_ref[...].astype(BF16)) + b_ref[...]


def _adaln(cc, w_ada, b_ada):
    depth, d, nd = w_ada.shape
    return pl.pallas_call(
        _adaln_body,
        grid=(depth, nd // TN_ADA),
        in_specs=[
            pl.BlockSpec((MOD_ROWS, d), lambda l, j: (0, 0)),
            pl.BlockSpec((None, d, TN_ADA), lambda l, j: (l, 0, j)),
            pl.BlockSpec((None, 1, TN_ADA), lambda l, j: (l, 0, j)),
        ],
        out_specs=pl.BlockSpec((None, MOD_ROWS, TN_ADA), lambda l, j: (l, 0, j)),
        out_shape=jax.ShapeDtypeStruct((depth, MOD_ROWS, nd), F32),
        compiler_params=_params(("arbitrary", "arbitrary")),
        name="adaln",
    )(cc, w_ada, b_ada.reshape(depth, 1, nd))


def _mod_spec(layer, chunk, d, tm, seq):
    return pl.BlockSpec((None, None, 1, d), lambda i, *_: (layer, (i * tm) // seq, 0, chunk))


def _gain_spec(layer, idx, d):
    return pl.BlockSpec((None, None, 1, d), lambda i, *_: (layer, idx, 0, 0))


def _norm_mod_body(x_ref, g_ref, shift_ref, scale_ref, o_ref):
    x = x_ref[...]
    y = x * _rstd(x) * g_ref[...]
    o_ref[...] = (y * (1.0 + scale_ref[...]) + shift_ref[...]).astype(BF16)


def _norm_mod(xs, mod4, gains4, layer, seq):
    t, d = xs.shape
    tm = 256
    return pl.pallas_call(
        _norm_mod_body,
        grid=(t // tm,),
        in_specs=[
            pl.BlockSpec((tm, d), lambda i: (i, 0)),
            _gain_spec(layer, 0, d),
            _mod_spec(layer, 0, d, tm, seq),
            _mod_spec(layer, 1, d, tm, seq),
        ],
        out_specs=pl.BlockSpec((tm, d), lambda i: (i, 0)),
        out_shape=jax.ShapeDtypeStruct((t, d), BF16),
        compiler_params=_params(("arbitrary",)),
        name="norm_mod",
    )(xs, gains4, mod4, mod4)


def _ffn_up_body(h_ref, wg_ref, wu_ref, o_ref):
    h = h_ref[...]
    a = _dot(h, wg_ref[...])
    b = _dot(h, wu_ref[...])
    o_ref[...] = (_silu(a) * b).astype(BF16)


def _ffn_up(h, w_gate, w_up, layer, which):
    t, d = h.shape
    dff = w_gate.shape[-1]
    wspec = pl.BlockSpec((None, None, d, TN_UP), lambda i, j: (layer, which, 0, j))
    return pl.pallas_call(
        _ffn_up_body,
        grid=(t // TM_UP, dff // TN_UP),
        in_specs=[pl.BlockSpec((TM_UP, d), lambda i, j: (i, 0)), wspec, wspec],
        out_specs=pl.BlockSpec((TM_UP, TN_UP), lambda i, j: (i, j)),
        out_shape=jax.ShapeDtypeStruct((t, dff), BF16),
        compiler_params=_params(("arbitrary", "arbitrary")),
        name="ffn_up",
    )(h, w_gate, w_up)


def _down_epi_body(a_ref, w_ref, x_ref, gate_ref, gpost_ref, *rest, coef, nk, tm, with_next):
    if with_next:
        gnext_ref, shift_ref, scale_ref, xo_ref, ho_ref = rest
    else:
        (xo_ref,) = rest
    k = pl.program_id(1)
    part = _dot(a_ref[...], w_ref[...])

    @pl.when(k == 0)
    def _():
        xo_ref[...] = part

    @pl.when(k != 0)
    def _():
        xo_ref[...] += part

    @pl.when(k == nk - 1)
    def _():
        gate = coef * gate_ref[...]
        gpost = gpost_ref[...]
        if with_next:
            gnext = gnext_ref[...]
            shift = shift_ref[...]
            scale1 = 1.0 + scale_ref[...]

        def chunk(c, carry):
            rows = pl.ds(pl.multiple_of(c * EPI_ROWS, EPI_ROWS), EPI_ROWS)
            y = xo_ref[rows, :]
            xn = x_ref[rows, :] + gate * (y * _rstd(y) * gpost)
            xo_ref[rows, :] = xn
            if with_next:
                hn = xn * _rstd(xn) * gnext
                ho_ref[rows, :] = (hn * scale1 + shift).astype(BF16)
            return carry

        lax.fori_loop(0, tm // EPI_ROWS, chunk, 0)


def _down_epi(a, w, wsel, xs, rows, seq, mod4, gains4, layer, gate_chunk, gpost_idx, coef,
              nxt=None):
    kdim = a.shape[1]
    d = xs.shape[1]
    tm, tk = TM_DOWN, TK_DOWN
    nk = kdim // tk
    nlead = len(wsel)
    wspec = pl.BlockSpec((None,) * nlead + (tk, d), lambda i, k: tuple(wsel) + (k, 0))
    in_specs = [
        pl.BlockSpec((tm, tk), lambda i, k: (i, k)),
        wspec,
        pl.BlockSpec((tm, d), lambda i, k: (i, 0)),
        _mod_spec(layer, gate_chunk, d, tm, seq),
        _gain_spec(layer, gpost_idx, d),
    ]
    args = [a, w, xs, mod4, gains4]
    out_specs = [pl.BlockSpec((tm, d), lambda i, k: (i, 0))]
    out_shape = [jax.ShapeDtypeStruct((rows, d), F32)]
    if nxt is not None:
        mod_n, gains_n, layer_n, gain_idx, shift_chunk, scale_chunk = nxt
        in_specs += [
            _gain_spec(layer_n, gain_idx, d),
            _mod_spec(layer_n, shift_chunk, d, tm, seq),
            _mod_spec(layer_n, scale_chunk, d, tm, seq),
        ]
        args += [gains_n, mod_n, mod_n]
        out_specs.append(pl.BlockSpec((tm, d), lambda i, k: (i, 0)))
        out_shape.append(jax.ShapeDtypeStruct((rows, d), BF16))
    body = functools.partial(_down_epi_body, coef=coef, nk=nk, tm=tm, with_next=nxt is not None)
    out = pl.pallas_call(
        body,
        grid=(rows // tm, nk),
        in_specs=in_specs,
        out_specs=out_specs,
        out_shape=out_shape,
        compiler_params=_params(("arbitrary", "arbitrary")),
        name="down_epi",
    )(*args)
    return (out[0], out[1]) if nxt is not None else (out[0], None)


def _proj_body(h_ref, w_ref, cos_ref, sin_ref, o_ref, *, n_lat_tiles):
    i = pl.program_id(0)
    j = pl.program_id(1)
    acc = _dot(h_ref[...], w_ref[...])
    in_rope_cols = functools.reduce(
        jnp.logical_or,
        [jnp.logical_and(j >= lo // TN_PROJ, j < hi // TN_PROJ) for lo, hi in ROPE_RANGES])
    is_rope = jnp.logical_and(i < n_lat_tiles, in_rope_cols)

    @pl.when(is_rope)
    def _():
        lane = lax.broadcasted_iota(jnp.int32, acc.shape, 1)
        quarter = HEAD_DIM // 4
        partner = jnp.where((lane % (2 * quarter)) < quarter,
                            pltpu.roll(acc, TN_PROJ - quarter, 1),
                            pltpu.roll(acc, quarter, 1))
        o_ref[...] = (acc * cos_ref[...] + partner * sin_ref[...]).astype(BF16)

    @pl.when(jnp.logical_not(is_rope))
    def _():
        o_ref[...] = acc.astype(BF16)


def _proj(h, w_in, layer, cos_t, sin_t, seq, t_lat):
    t, d = h.shape
    width = w_in.shape[-1]
    pos_tiles = seq // TM_PROJ
    tspec = pl.BlockSpec((TM_PROJ, TN_PROJ), lambda i, j: (i % pos_tiles, 0))
    return pl.pallas_call(
        functools.partial(_proj_body, n_lat_tiles=t_lat // TM_PROJ),
        grid=(t // TM_PROJ, width // TN_PROJ),
        in_specs=[
            pl.BlockSpec((TM_PROJ, d), lambda i, j: (i, 0)),
            pl.BlockSpec((None, d, TN_PROJ), lambda i, j: (layer, 0, j)),
            tspec, tspec,
        ],
        out_specs=pl.BlockSpec((TM_PROJ, TN_PROJ), lambda i, j: (i, j)),
        out_shape=jax.ShapeDtypeStruct((t, width), BF16),
        compiler_params=_params(("arbitrary", "arbitrary")),
        name="proj_rope",
    )(h, w_in, cos_t, sin_t)


def _rope_tables(n):
    t = jnp.arange(n, dtype=jnp.int32)
    row = (t // GRID_W).astype(F32)
    col = (t % GRID_W).astype(F32)
    half = HEAD_DIM // 2
    quarter = HEAD_DIM // 4
    inv = ROPE_BASE ** (-jnp.arange(0, half, 2, dtype=F32) / half)
    ang = jnp.concatenate([row[:, None] * inv, col[:, None] * inv], axis=-1)
    lane = np.arange(HEAD_DIM)
    src = quarter * (lane // half) + lane % quarter
    sign = np.where(lane % half < quarter, -1.0, 1.0).astype(np.float32)
    cos_h = jnp.cos(ang)[:, src]
    sin_h = jnp.sin(ang)[:, src] * sign
    reps = TN_PROJ // HEAD_DIM
    return jnp.tile(cos_h, (1, reps)), jnp.tile(sin_h, (1, reps))


def _na_classes(rows):
    edge = NA_WIN_ROWS // 2
    return list(range(edge)) + [edge] + list(range(rows - edge, rows))


def _na_bias_table(rpb, rows):
    kh, kw = NA_WIN_ROWS, NA_WIN_COLS
    qc = np.arange(GRID_W)
    kc = np.arange(GRID_W)
    col_start = np.clip(qc - kw // 2, 0, GRID_W - kw)
    ok = (kc[None, :] >= col_start[:, None]) & (kc[None, :] < col_start[:, None] + kw)
    dc = np.clip(kc[None, :] - qc[:, None] + NA_WIN_COLS - 1, 0, 2 * NA_WIN_COLS - 2)
    reps = np.array(_na_classes(rows))
    rs = np.clip(reps - kh // 2, 0, rows - kh)
    dr = rs[:, None] + np.arange(kh)[None, :] - reps[:, None] + NA_WIN_ROWS - 1
    b = rpb.astype(F32)[:, dr[:, :, None, None], dc[None, None, :, :]]
    b = jnp.where(ok[None, None, None], b, NEG_INF)
    h = rpb.shape[0]
    return b.transpose(0, 1, 3, 2, 4).reshape(h, len(reps), GRID_W, kh * GRID_W)


def _softmax_parts(parts, extra=None):
    m = functools.reduce(jnp.maximum, [jnp.max(p, axis=-1, keepdims=True) for p in parts])
    if extra is not None:
        m = jnp.maximum(m, extra)
    es = [jnp.exp(p - m) for p in parts]
    l = functools.reduce(jnp.add, [jnp.sum(e, axis=-1, keepdims=True) for e in es])
    if extra is not None:
        l = l + jnp.exp(extra - m)
    inv = 1.0 / l
    return [e * inv for e in es]


def _na_body(q_ref, k_ref, v_ref, kx_ref, vx_ref, bias_ref, o_ref, *, rows):
    kh = NA_WIN_ROWS
    edge = kh // 2
    kx = kx_ref[...]
    vx = vx_ref[...]

    def row_fn(r, carry):
        rs = jnp.clip(r - edge, 0, rows - kh)
        cls = jnp.where(r < edge, r, jnp.where(r >= rows - edge, r - (rows - 2 * edge - 1), edge))
        qrows = pl.ds(pl.multiple_of(r * GRID_W, GRID_W), GRID_W)
        krows = pl.ds(pl.multiple_of(rs * GRID_W, GRID_W), kh * GRID_W)
        q = q_ref[qrows, :]
        s_loc = _dot_t(q, k_ref[krows, :]) * ATT_SCALE + bias_ref[cls]
        s_ctx = _dot_t(q, kx) * ATT_SCALE
        p_loc, p_ctx = _softmax_parts([s_loc, s_ctx])
        o = _dot(p_loc.astype(BF16), v_ref[krows, :]) + _dot(p_ctx.astype(BF16), vx)
        o_ref[qrows, :] = o.astype(BF16)
        return carry

    lax.fori_loop(0, rows, row_fn, 0)


def _na(qkv, bias_tbl, batch, seq, ctx_len, t):
    rows = seq // GRID_W
    hd = HEAD_DIM
    ctx_blk0 = batch * seq // ctx_len
    ncls = bias_tbl.shape[1]

    def col(off):
        return off // hd

    return pl.pallas_call(
        functools.partial(_na_body, rows=rows),
        grid=(batch, NA_HEADS),
        in_specs=[
            pl.BlockSpec((seq, hd), lambda b, h: (b, col(OFF_QA) + h)),
            pl.BlockSpec((seq, hd), lambda b, h: (b, col(OFF_KA) + h)),
            pl.BlockSpec((seq, hd), lambda b, h: (b, col(OFF_VA) + h)),
            pl.BlockSpec((ctx_len, hd), lambda b, h: (ctx_blk0 + b, col(OFF_KA) + h)),
            pl.BlockSpec((ctx_len, hd), lambda b, h: (ctx_blk0 + b, col(OFF_VA) + h)),
            pl.BlockSpec((None, ncls, GRID_W, NA_WIN_ROWS * GRID_W), lambda b, h: (h, 0, 0, 0)),
        ],
        out_specs=pl.BlockSpec((seq, hd), lambda b, h: (b, col(OFF_OA) + h)),
        out_shape=jax.ShapeDtypeStruct((t, ATT_WIDTH), BF16),
        compiler_params=_params(("arbitrary", "arbitrary")),
        name="na_attn",
    )(qkv, qkv, qkv, qkv, qkv, bias_tbl)


def _sw_body(sink_ref, q_ref, k_ref, v_ref, kx_ref, vx_ref, att_in_ref, o_ref, *, seq):
    del att_in_ref
    kv = pl.program_id(1)
    blk = SW_BLOCK
    span = 3 * blk
    g = SW_GROUP
    kx = kx_ref[...]
    vx = vx_ref[...]
    grp = lax.broadcasted_iota(jnp.int32, (g * blk, 1), 0) // blk
    sink_col = jnp.zeros((g * blk, 1), F32)
    for gi in range(g):
        sink_col = jnp.where(grp == gi, sink_ref[kv * g + gi], sink_col)
    qoff = lax.broadcasted_iota(jnp.int32, (g * blk, span), 0) % blk
    koff = lax.broadcasted_iota(jnp.int32, (g * blk, span), 1)

    def blk_fn(n, carry):
        start = jnp.clip((n - 1) * blk, 0, seq - span)
        qrows = pl.ds(pl.multiple_of(n * blk, blk), blk)
        krows = pl.ds(pl.multiple_of(start, blk), span)
        qb = q_ref[qrows, :]
        qs = jnp.concatenate([qb[:, gi * HEAD_DIM:(gi + 1) * HEAD_DIM] for gi in range(g)], axis=0)
        s_loc = _dot_t(qs, k_ref[krows, :]) * ATT_SCALE
        dist = (n * blk + qoff) - (start + koff)
        s_loc = jnp.where(jnp.abs(dist) <= SW_WINDOW, s_loc, NEG_INF)
        s_ctx = _dot_t(qs, kx) * ATT_SCALE
        p_loc, p_ctx = _softmax_parts([s_loc, s_ctx], extra=sink_col)
        o = _dot(p_loc.astype(BF16), v_ref[krows, :]) + _dot(p_ctx.astype(BF16), vx)
        for gi in range(g):
            o_ref[qrows, gi * HEAD_DIM:(gi + 1) * HEAD_DIM] = o[gi * blk:(gi + 1) * blk].astype(BF16)
        return carry

    lax.fori_loop(0, seq // blk, blk_fn, 0)


def _sw(qkv, att, sink, batch, seq, ctx_len):
    hd = HEAD_DIM
    gw = SW_GROUP * hd
    ctx_blk0 = batch * seq // ctx_len
    return pl.pallas_call(
        functools.partial(_sw_body, seq=seq),
        grid=(batch, SW_KV_HEADS),
        in_specs=[
            pl.BlockSpec(memory_space=pltpu.SMEM),
            pl.BlockSpec((seq, gw), lambda b, k: (b, OFF_QB // gw + k)),
            pl.BlockSpec((seq, hd), lambda b, k: (b, OFF_KB // hd + k)),
            pl.BlockSpec((seq, hd), lambda b, k: (b, OFF_VB // hd + k)),
            pl.BlockSpec((ctx_len, hd), lambda b, k: (ctx_blk0 + b, OFF_KB // hd + k)),
            pl.BlockSpec((ctx_len, hd), lambda b, k: (ctx_blk0 + b, OFF_VB // hd + k)),
            pl.BlockSpec(memory_space=pl.ANY),
        ],
        out_specs=pl.BlockSpec((seq, gw), lambda b, k: (b, OFF_OB // gw + k)),
        out_shape=jax.ShapeDtypeStruct(att.shape, att.dtype),
        input_output_aliases={6: 0},
        compiler_params=_params(("arbitrary", "arbitrary")),
        name="sw_attn",
    )(sink, qkv, qkv, qkv, qkv, qkv, att)


def _df_lambda(lam_ref, lambda_init):
    lv = lam_ref[...]
    s01 = jnp.sum(lv[0:1] * lv[1:2], axis=-1, keepdims=True)
    s23 = jnp.sum(lv[2:3] * lv[3:4], axis=-1, keepdims=True)
    return jnp.exp(s01) - jnp.exp(s23) + lambda_init


def _df_core(q, k_parts, v_parts, lam, subln, lambda_init):
    hd = HEAD_DIM
    probs = []
    for c in range(2):
        qc = q[:, c * hd:(c + 1) * hd]
        scores = [_dot_t(qc, kp[:, c * hd:(c + 1) * hd]) * ATT_SCALE for kp in k_parts]
        probs.append(_softmax_parts(scores))
    o = None
    for i, vp in enumerate(v_parts):
        a = (probs[0][i] - lam * probs[1][i]).astype(BF16)
        term = _dot(a, vp[...])
        o = term if o is None else o + term
    return o * _rstd(o) * subln * (1.0 - lambda_init)


def _df_body(lam_ref, subln_ref, q_ref, k_ref, v_ref, kx_ref, vx_ref, att_in_ref, o_ref, *,
             lambda_init):
    del att_in_ref
    lam = _df_lambda(lam_ref, lambda_init)
    o = _df_core(q_ref[...], [k_ref, kx_ref], [v_ref, vx_ref], lam, subln_ref[...], lambda_init)
    o_ref[...] = o.astype(BF16)


def _df(qkv, att, lam4, subln4, layer, lambda_init, batch, seq, ctx_len):
    w = 2 * HEAD_DIM
    nqb = seq // DF_Q_BLOCK
    ctx_blk0 = batch * seq // ctx_len
    return pl.pallas_call(
        functools.partial(_df_body, lambda_init=lambda_init),
        grid=(batch, DF_HEADS, nqb),
        in_specs=[
            pl.BlockSpec((None, 4, HEAD_DIM), lambda b, h, n: (layer, 0, 0)),
            pl.BlockSpec((None, 1, w), lambda b, h, n: (layer, 0, 0)),
            pl.BlockSpec((DF_Q_BLOCK, w), lambda b, h, n: (b * nqb + n, OFF_QC // w + h)),
            pl.BlockSpec((seq, w), lambda b, h, n: (b, OFF_KC // w + h)),
            pl.BlockSpec((seq, w), lambda b, h, n: (b, OFF_VC // w + h)),
            pl.BlockSpec((ctx_len, w), lambda b, h, n: (ctx_blk0 + b, OFF_KC // w + h)),
            pl.BlockSpec((ctx_len, w), lambda b, h, n: (ctx_blk0 + b, OFF_VC // w + h)),
            pl.BlockSpec(memory_space=pl.ANY),
        ],
        out_specs=pl.BlockSpec((DF_Q_BLOCK, w), lambda b, h, n: (b * nqb + n, OFF_OC // w + h)),
        out_shape=jax.ShapeDtypeStruct(att.shape, att.dtype),
        input_output_aliases={7: 0},
        compiler_params=_params(("arbitrary", "arbitrary", "arbitrary")),
        name="df_attn",
    )(lam4, subln4, qkv, qkv, qkv, qkv, qkv, att)


def _ctx_body(sink_ref, lam_ref, subln_ref, x_ref, att_in_ref, o_ref, *, lambda_init):
    del att_in_ref
    hd = HEAD_DIM

    def cols(off, width=hd):
        return x_ref[:, off:off + width]

    for h in range(NA_HEADS):
        s = _dot_t(cols(OFF_QA + h * hd), cols(OFF_KA + h * hd)) * ATT_SCALE
        (p,) = _softmax_parts([s])
        o = _dot(p.astype(BF16), cols(OFF_VA + h * hd))
        o_ref[:, OFF_OA + h * hd:OFF_OA + (h + 1) * hd] = o.astype(BF16)

    for hq in range(SW_HEADS):
        kvh = hq // SW_GROUP
        s = _dot_t(cols(OFF_QB + hq * hd), cols(OFF_KB + kvh * hd)) * ATT_SCALE
        sink = jnp.full((s.shape[0], 1), sink_ref[hq], F32)
        (p,) = _softmax_parts([s], extra=sink)
        o = _dot(p.astype(BF16), cols(OFF_VB + kvh * hd))
        o_ref[:, OFF_OB + hq * hd:OFF_OB + (hq + 1) * hd] = o.astype(BF16)

    lam = _df_lambda(lam_ref, lambda_init)
    subln = subln_ref[...]
    for h in range(DF_HEADS):
        w = 2 * hd
        o = _df_core(cols(OFF_QC + h * w, w), [cols(OFF_KC + h * w, w)], [cols(OFF_VC + h * w, w)],
                     lam, subln, lambda_init)
        o_ref[:, OFF_OC + h * w:OFF_OC + (h + 1) * w] = o.astype(BF16)


def _ctx_attn(qkv, att, sink, lam4, subln4, layer, lambda_init, batch, seq, ctx_len):
    ctx_blk0 = batch * seq // ctx_len
    return pl.pallas_call(
        functools.partial(_ctx_body, lambda_init=lambda_init),
        grid=(batch,),
        in_specs=[
            pl.BlockSpec(memory_space=pltpu.SMEM),
            pl.BlockSpec((None, 4, HEAD_DIM), lambda b: (layer, 0, 0)),
            pl.BlockSpec((None, 1, 2 * HEAD_DIM), lambda b: (layer, 0, 0)),
            pl.BlockSpec((ctx_len, IN_WIDTH), lambda b: (ctx_blk0 + b, 0)),
            pl.BlockSpec(memory_space=pl.ANY),
        ],
        out_specs=pl.BlockSpec((ctx_len, ATT_WIDTH), lambda b: (ctx_blk0 + b, 0)),
        out_shape=jax.ShapeDtypeStruct(att.shape, att.dtype),
        input_output_aliases={4: 0},
        compiler_params=_params(("arbitrary",)),
        name="ctx_attn",
    )(sink, lam4, subln4, qkv, att)


def _merge_body(h_ref, att_ref, wg_ref, bg_ref, wb_ref, o_ref, *, branch_width):
    h = h_ref[...]
    acc = None
    for i in range(3):
        gate = jax.nn.sigmoid(_dot(h, wg_ref[i]) + bg_ref[i])
        br = _dot(att_ref[:, i * branch_width:(i + 1) * branch_width], wb_ref[i])
        acc = gate * br if acc is None else acc + gate * br
    o_ref[...] = acc.astype(BF16)


def _merge(h, att, w_gate, b_gate4, w_branch, layer, rows):
    d = h.shape[1]
    nb, bw = w_branch.shape[1], w_branch.shape[2]
    tm, tn = TM_MERGE, TN_MERGE
    return pl.pallas_call(
        functools.partial(_merge_body, branch_width=bw),
        grid=(rows // tm, d // tn),
        in_specs=[
            pl.BlockSpec((tm, d), lambda i, j: (i, 0)),
            pl.BlockSpec((tm, nb * bw), lambda i, j: (i, 0)),
            pl.BlockSpec((None, nb, d, tn), lambda i, j: (layer, 0, 0, j)),
            pl.BlockSpec((None, nb, 1, tn), lambda i, j: (layer, 0, 0, j)),
            pl.BlockSpec((None, nb, bw, tn), lambda i, j: (layer, 0, 0, j)),
        ],
        out_specs=pl.BlockSpec((tm, tn), lambda i, j: (i, j)),
        out_shape=jax.ShapeDtypeStruct((rows, d), BF16),
        compiler_params=_params(("arbitrary", "arbitrary")),
        name="merge",
    )(h, att, w_gate, b_gate4, w_branch)


@jax.jit
def _forward(x, c, ctx, c_ctx, w_ada, b_ada, norm_g, w_ffn_gate, w_ffn_up, w_ffn_down,
             w_in, na_rpb, sw_sink, df_lambda, df_subln_g, w_branch, w_gate, b_gate, w_out):
    batch, seq, d = x.shape
    ctx_len = ctx.shape[1]
    depth = w_ada.shape[0]
    t_lat = batch * seq
    t = t_lat + batch * ctx_len
    assert seq % TM_UP == 0 and (batch * ctx_len) % TM_UP == 0 and batch + 1 <= MOD_ROWS
    assert seq % ctx_len == 0 and ctx_len % SW_BLOCK == 0

    xs = jnp.concatenate([x.reshape(t_lat, d), ctx.reshape(batch * ctx_len, d)], axis=0)
    cc = jnp.zeros((MOD_ROWS, d), F32).at[:batch].set(c).at[batch].set(c_ctx)
    mod4 = _adaln(cc, w_ada, b_ada).reshape(depth, MOD_ROWS, 1, N_MOD * d)
    gains4 = norm_g.reshape(depth, norm_g.shape[1], 1, d)

    wg16 = w_ffn_gate.astype(BF16)
    wu16 = w_ffn_up.astype(BF16)
    wd16 = w_ffn_down.astype(BF16)
    win16 = w_in.astype(BF16)
    wbr16 = w_branch.astype(BF16)
    wgt16 = w_gate.astype(BF16)
    wout16 = w_out.astype(BF16)
    b_gate4 = b_gate.reshape(depth, b_gate.shape[1], 1, d)
    lam4 = df_lambda.astype(F32)
    subln4 = df_subln_g.reshape(depth, 1, -1)
    cos_t, sin_t = _rope_tables(seq)

    h = _norm_mod(xs, mod4, gains4, 0, seq)
    for l in range(depth):
        last = l == depth - 1
        lambda_init = 0.8 - 0.6 * math.exp(-0.3 * l)
        u = _ffn_up(h, wg16, wu16, l, 0)
        xs, h = _down_epi(u, wd16, (l, 0), xs, t, seq, mod4, gains4, l, 2, 1, 0.5,
                          nxt=(mod4, gains4, l, 2, 3, 4))
        qkv = _proj(h, win16, l, cos_t, sin_t, seq, t_lat)
        att = _na(qkv, _na_bias_table(na_rpb[l], seq // GRID_W), batch, seq, ctx_len, t)
        att = _sw(qkv, att, sw_sink[l], batch, seq, ctx_len)
        att = _df(qkv, att, lam4, subln4, l, lambda_init, batch, seq, ctx_len)
        if not last:
            att = _ctx_attn(qkv, att, sw_sink[l], lam4, subln4, l, lambda_init, batch, seq, ctx_len)
        rows = t_lat if last else t
        merged = _merge(h, att, wgt16, b_gate4, wbr16, l, rows)
        xs, h = _down_epi(merged, wout16, (l,), xs, rows, seq, mod4, gains4, l, 5, 3, 1.0,
                          nxt=(mod4, gains4, l, 4, 6, 7))
        u = _ffn_up(h, wg16, wu16, l, 1)
        nxt = None if last else (mod4, gains4, l + 1, 0, 0, 1)
        xs, h = _down_epi(u, wd16, (l, 1), xs, rows, seq, mod4, gains4, l, 8, 5, 0.5, nxt=nxt)
    return xs.reshape(batch, seq, d)


def kernel(x, c, ctx, c_ctx, w_ada, b_ada, norm_g, w_ffn_gate, w_ffn_up, w_ffn_down,
           w_in, na_rpb, sw_sink, df_lambda, df_subln_g, w_branch, w_gate, b_gate, w_out):
    return _forward(x, c, ctx, c_ctx, w_ada, b_ada, norm_g, w_ffn_gate, w_ffn_up, w_ffn_down,
                    w_in, na_rpb, sw_sink, df_lambda, df_subln_g, w_branch, w_gate, b_gate, w_out)
```

```python
import functools
import math

import numpy as np
import jax
import jax.numpy as jnp
from jax import lax
from jax.experimental import pallas as pl
from jax.experimental.pallas import tpu as pltpu

F32 = jnp.float32
BF16 = jnp.bfloat16

GRID_W = 64
HEAD_DIM = 128
NA_HEADS = 8
NA_WIN_ROWS = 8
NA_WIN_COLS = 16
SW_HEADS = 8
SW_KV_HEADS = 2
SW_GROUP = SW_HEADS // SW_KV_HEADS
SW_WINDOW = 128
SW_BLOCK = 128
DF_HEADS = 4
DF_Q_BLOCK = 128
N_MOD = 9
ROPE_BASE = 10000.0
EPS = 1e-6
NEG_INF = -1e30
LOG2E = math.log2(math.e)
Q_SCALE = HEAD_DIM ** -0.5 * LOG2E

OFF_QA, OFF_KA, OFF_VA = 0, 1024, 2048
OFF_QB, OFF_KB, OFF_VB = 3072, 4096, 4352
OFF_QC, OFF_KC, OFF_VC = 4608, 5632, 6656
IN_WIDTH = 7680
OFF_OA, OFF_OB, OFF_OC = 0, 1024, 2048
ATT_WIDTH = 3072

V7X_LANES = 128
MOD_ROWS = 8
V7X_VMEM_LIMIT = 58 * 1024 * 1024

TM_UP, TN_UP = 1024, 512
TM_DOWN, TK_DOWN = 512, 512
TM_PROJ, TN_PROJ = 1024, 512
DF_Q_STEP = 2 * DF_Q_BLOCK
NA_Q_ROWS = 4
NA_KEY_ROWS = 12
NA_UNROLL = 2
SW_UNROLL = 2
TM_MERGE, TN_MERGE = 1024, 256
TN_ADA = 512
EPI_ROWS = 16
EPI_UNROLL = 8


def _params(sem):
    return pltpu.CompilerParams(dimension_semantics=sem, vmem_limit_bytes=V7X_VMEM_LIMIT)


def _dot(a, b):
    return jnp.dot(a, b, preferred_element_type=F32)


def _dot_t(a, b):
    return lax.dot_general(a, b, (((1,), (1,)), ((), ())), preferred_element_type=F32)


def _rstd(y):
    return lax.rsqrt(jnp.mean(y * y, axis=-1, keepdims=True) + EPS)


def _silu(a):
    return a * jax.nn.sigmoid(a)


def _adaln_body(c_ref, w_ref, b_ref, o_ref):
    s = _silu(c_ref[...]).astype(BF16)
    o_ref[...] = _dot(s, w_ref[...].astype(BF16)) + b_ref[...]


def _adaln(cc, w_ada, b_ada):
    depth, d, nd = w_ada.shape
    return pl.pallas_call(
        _adaln_body,
        grid=(depth, nd // TN_ADA),
        in_specs=[
            pl.BlockSpec((MOD_ROWS, d), lambda l, j: (0, 0)),
            pl.BlockSpec((None, d, TN_ADA), lambda l, j: (l, 0, j)),
            pl.BlockSpec((None, 1, TN_ADA), lambda l, j: (l, 0, j)),
        ],
        out_specs=pl.BlockSpec((None, MOD_ROWS, TN_ADA), lambda l, j: (l, 0, j)),
        out_shape=jax.ShapeDtypeStruct((depth, MOD_ROWS, nd), F32),
        compiler_params=_params(("arbitrary", "arbitrary")),
        name="adaln",
    )(cc, w_ada, b_ada.reshape(depth, 1, nd))


def _mod_spec(layer, chunk, d, tm, seq):
    return pl.BlockSpec((None, None, 1, d), lambda i, *_: (layer, (i * tm) // seq, 0, chunk))


def _gain_spec(layer, idx, d):
    return pl.BlockSpec((None, None, 1, d), lambda i, *_: (layer, idx, 0, 0))


def _norm_mod_body(x_ref, g_ref, shift_ref, scale_ref, o_ref):
    x = x_ref[...]
    y = x * _rstd(x) * g_ref[...]
    o_ref[...] = (y * (1.0 + scale_ref[...]) + shift_ref[...]).astype(BF16)


def _norm_mod(xs, mod4, gains4, layer, seq):
    t, d = xs.shape
    tm = 256
    return pl.pallas_call(
        _norm_mod_body,
        grid=(t // tm,),
        in_specs=[
            pl.BlockSpec((tm, d), lambda i: (i, 0)),
            _gain_spec(layer, 0, d),
            _mod_spec(layer, 0, d, tm, seq),
            _mod_spec(layer, 1, d, tm, seq),
        ],
        out_specs=pl.BlockSpec((tm, d), lambda i: (i, 0)),
        out_shape=jax.ShapeDtypeStruct((t, d), BF16),
        compiler_params=_params(("arbitrary",)),
        name="norm_mod",
    )(xs, gains4, mod4, mod4)


def _ffn_up_body(h_ref, wg_ref, wu_ref, o_ref):
    h = h_ref[...]
    a = _dot(h, wg_ref[...])
    b = _dot(h, wu_ref[...])
    o_ref[...] = (_silu(a) * b).astype(BF16)


def _ffn_up(h, w_gate, w_up, layer, which):
    t, d = h.shape
    dff = w_gate.shape[-1]
    wspec = pl.BlockSpec((None, None, d, TN_UP), lambda i, j: (layer, which, 0, j))
    return pl.pallas_call(
        _ffn_up_body,
        grid=(t // TM_UP, dff // TN_UP),
        in_specs=[pl.BlockSpec((TM_UP, d), lambda i, j: (i, 0)), wspec, wspec],
        out_specs=pl.BlockSpec((TM_UP, TN_UP), lambda i, j: (i, j)),
        out_shape=jax.ShapeDtypeStruct((t, dff), BF16),
        compiler_params=_params(("arbitrary", "arbitrary")),
        name="ffn_up",
    )(h, w_gate, w_up)


def _down_epi_body(a_ref, w_ref, x_ref, gate_ref, gpost_ref, *rest, coef, nk, tm, with_next):
    if with_next:
        gnext_ref, shift_ref, scale_ref, xo_ref, ho_ref, stat_ref = rest
    else:
        xo_ref, stat_ref = rest
    k = pl.program_id(1)
    d = xo_ref.shape[-1]

    @pl.when(k == 0)
    def _():
        xo_ref[...] = _dot(a_ref[...], w_ref[...])

    @pl.when(k != 0)
    def _():
        xo_ref[...] += _dot(a_ref[...], w_ref[...])

    @pl.when(k == nk - 1)
    def _():
        gate_gain = (coef * gate_ref[...]) * gpost_ref[...]
        if with_next:
            next_gain = gnext_ref[...] * (1.0 + scale_ref[...])
            shift = shift_ref[...]

        lanes = stat_ref.shape[-1]

        def chunk_rows(c):
            return pl.ds(pl.multiple_of(c * EPI_ROWS, EPI_ROWS), EPI_ROWS)

        def put_stat(slot, rows, val):
            stat_ref[slot, rows, :] = jnp.broadcast_to(val, (EPI_ROWS, lanes))

        def get_stat(slot, rows):
            return pltpu.repeat(stat_ref[slot, rows, :], d // lanes, axis=1)

        def pass_norm(c, carry):
            rows = chunk_rows(c)
            put_stat(0, rows, _rstd(xo_ref[rows, :]))
            return carry

        def pass_residual(c, carry):
            rows = chunk_rows(c)
            xn = x_ref[rows, :] + (xo_ref[rows, :] * get_stat(0, rows)) * gate_gain
            xo_ref[rows, :] = xn
            if with_next:
                put_stat(1, rows, _rstd(xn))
            return carry

        def pass_modulate(c, carry):
            rows = chunk_rows(c)
            hn = (xo_ref[rows, :] * get_stat(1, rows)) * next_gain + shift
            ho_ref[rows, :] = hn.astype(BF16)
            return carry

        n_chunks = tm // EPI_ROWS
        lax.fori_loop(0, n_chunks, pass_norm, 0, unroll=EPI_UNROLL)
        lax.fori_loop(0, n_chunks, pass_residual, 0, unroll=EPI_UNROLL)
        if with_next:
            lax.fori_loop(0, n_chunks, pass_modulate, 0, unroll=EPI_UNROLL)


def _down_epi(a, w, wsel, xs, rows, seq, mod4, gains4, layer, gate_chunk, gpost_idx, coef,
              nxt=None):
    kdim = a.shape[1]
    d = xs.shape[1]
    tm, tk = TM_DOWN, TK_DOWN
    nk = kdim // tk
    nlead = len(wsel)
    wspec = pl.BlockSpec((None,) * nlead + (tk, d), lambda i, k: tuple(wsel) + (k, 0))
    in_specs = [
        pl.BlockSpec((tm, tk), lambda i, k: (i, k)),
        wspec,
        pl.BlockSpec((tm, d), lambda i, k: (i, 0)),
        _mod_spec(layer, gate_chunk, d, tm, seq),
        _gain_spec(layer, gpost_idx, d),
    ]
    args = [a, w, xs, mod4, gains4]
    out_specs = [pl.BlockSpec((tm, d), lambda i, k: (i, 0))]
    out_shape = [jax.ShapeDtypeStruct((rows, d), F32)]
    if nxt is not None:
        mod_n, gains_n, layer_n, gain_idx, shift_chunk, scale_chunk = nxt
        in_specs += [
            _gain_spec(layer_n, gain_idx, d),
            _mod_spec(layer_n, shift_chunk, d, tm, seq),
            _mod_spec(layer_n, scale_chunk, d, tm, seq),
        ]
        args += [gains_n, mod_n, mod_n]
        out_specs.append(pl.BlockSpec((tm, d), lambda i, k: (i, 0)))
        out_shape.append(jax.ShapeDtypeStruct((rows, d), BF16))
    body = functools.partial(_down_epi_body, coef=coef, nk=nk, tm=tm, with_next=nxt is not None)
    out = pl.pallas_call(
        body,
        grid=(rows // tm, nk),
        in_specs=in_specs,
        out_specs=out_specs,
        out_shape=out_shape,
        scratch_shapes=[pltpu.VMEM((2, tm, V7X_LANES), F32)],
        compiler_params=_params(("arbitrary", "arbitrary")),
        name="down_epi",
    )(*args)
    return (out[0], out[1]) if nxt is not None else (out[0], None)


def _in_tiles(j, lo, hi):
    return jnp.logical_and(j >= lo // TN_PROJ, j < hi // TN_PROJ)


def _proj_body(h_ref, w_ref, cos_ref, sin_ref, o_ref, *, n_lat_tiles):
    i = pl.program_id(0)
    j = pl.program_id(1)
    hd = HEAD_DIM
    acc = _dot(h_ref[...], w_ref[...])
    is_q = _in_tiles(j, OFF_QA, OFF_KA) | _in_tiles(j, OFF_QB, OFF_KB) | _in_tiles(j, OFF_QC, OFF_KC)
    qscale = jnp.where(is_q, Q_SCALE, 1.0).astype(F32)
    latent = i < n_lat_tiles
    rope_full = latent & (_in_tiles(j, OFF_QB, OFF_KB) | _in_tiles(j, OFF_QC, OFF_VC))
    rope_half = latent & _in_tiles(j, OFF_KB, OFF_QC)

    def rope_heads(n_heads, scale=None):
        cos = cos_ref[...]
        sin = sin_ref[...]
        if scale is not None:
            cos = cos * scale
            sin = sin * scale
        lane = lax.broadcasted_iota(jnp.int32, cos.shape, 1)
        quarter = hd // 4
        first = (lane % (2 * quarter)) < quarter
        for hh in range(n_heads):
            a = acc[:, hh * hd:(hh + 1) * hd]
            partner = jnp.where(first, pltpu.roll(a, hd - quarter, 1), pltpu.roll(a, quarter, 1))
            o_ref[:, hh * hd:(hh + 1) * hd] = (a * cos + partner * sin).astype(BF16)

    @pl.when(rope_full)
    def _():
        rope_heads(TN_PROJ // hd, qscale)

    @pl.when(rope_half)
    def _():
        n_rope = (OFF_VB - OFF_KB) // hd
        rope_heads(n_rope)
        o_ref[:, n_rope * hd:] = acc[:, n_rope * hd:].astype(BF16)

    @pl.when(jnp.logical_not(rope_full | rope_half))
    def _():
        o_ref[...] = (acc * qscale).astype(BF16)


def _proj(h, w_in, layer, cos_t, sin_t, seq, t_lat):
    t, d = h.shape
    width = w_in.shape[-1]
    pos_tiles = seq // TM_PROJ
    assert OFF_KB % TN_PROJ == 0 and OFF_QC - OFF_KB == TN_PROJ
    tspec = pl.BlockSpec((TM_PROJ, HEAD_DIM), lambda i, j: (i % pos_tiles, 0))
    return pl.pallas_call(
        functools.partial(_proj_body, n_lat_tiles=t_lat // TM_PROJ),
        grid=(t // TM_PROJ, width // TN_PROJ),
        in_specs=[
            pl.BlockSpec((TM_PROJ, d), lambda i, j: (i, 0)),
            pl.BlockSpec((None, d, TN_PROJ), lambda i, j: (layer, 0, j)),
            tspec, tspec,
        ],
        out_specs=pl.BlockSpec((TM_PROJ, TN_PROJ), lambda i, j: (i, j)),
        out_shape=jax.ShapeDtypeStruct((t, width), BF16),
        compiler_params=_params(("arbitrary", "arbitrary")),
        name="proj_rope",
    )(h, w_in, cos_t, sin_t)


def _rope_tables(n):
    t = jnp.arange(n, dtype=jnp.int32)
    row = (t // GRID_W).astype(F32)
    col = (t % GRID_W).astype(F32)
    half = HEAD_DIM // 2
    inv = ROPE_BASE ** (-jnp.arange(0, half, 2, dtype=F32) / half)
    ang_r = row[:, None] * inv
    ang_c = col[:, None] * inv
    cos_h = jnp.concatenate([jnp.cos(ang_r)] * 2 + [jnp.cos(ang_c)] * 2, axis=-1)
    sin_h = jnp.concatenate([-jnp.sin(ang_r), jnp.sin(ang_r), -jnp.sin(ang_c), jnp.sin(ang_c)], axis=-1)
    return cos_h, sin_h


def _na_key_row_start(blk, rows):
    return np.clip(blk * NA_Q_ROWS - NA_WIN_ROWS // 2, 0, rows - NA_KEY_ROWS)


def _na_bias_table(rpb, rows):
    kh, kw = NA_WIN_ROWS, NA_WIN_COLS
    nblk = rows // NA_Q_ROWS
    assert rows % NA_Q_ROWS == 0 and nblk >= 3 and rows >= NA_KEY_ROWS
    qc = np.arange(GRID_W)
    kc = np.arange(GRID_W)
    col_start = np.clip(qc - kw // 2, 0, GRID_W - kw)
    col_ok = (kc[None, :] >= col_start[:, None]) & (kc[None, :] < col_start[:, None] + kw)
    dc = kc[None, :] - qc[:, None] + NA_WIN_COLS - 1
    col_hot = (dc[None] == np.arange(2 * NA_WIN_COLS - 1)[:, None, None]).astype(np.float32)
    n_dr = 2 * NA_WIN_ROWS - 1
    row_hot = np.zeros((3, NA_Q_ROWS, NA_KEY_ROWS, n_dr), np.float32)
    for cls, blk in enumerate((0, 1, nblk - 1)):
        ks = _na_key_row_start(blk, rows)
        for j in range(NA_Q_ROWS):
            r = blk * NA_Q_ROWS + j
            rs = np.clip(r - kh // 2, 0, rows - kh)
            for i in range(NA_KEY_ROWS):
                if rs <= ks + i < rs + kh:
                    row_hot[cls, j, i, ks + i - r + NA_WIN_ROWS - 1] = 1.0
    b = jnp.einsum('cjir,hrd,dqk->hcjqik', row_hot, rpb.astype(F32) * LOG2E, col_hot,
                   precision=lax.Precision.HIGHEST)
    ok = (row_hot.sum(-1) > 0)[:, :, None, :, None] & col_ok[None, None, :, None, :]
    b = jnp.where(ok[None], b, NEG_INF)
    return b.reshape(rpb.shape[0], 3, NA_Q_ROWS * GRID_W, NA_KEY_ROWS * GRID_W)


def _softmax_parts(parts, extra=None):
    m = functools.reduce(jnp.maximum, [jnp.max(p, axis=-1, keepdims=True) for p in parts])
    if extra is not None:
        m = jnp.maximum(m, extra)
    es = [jnp.exp2(p - m) for p in parts]
    l = functools.reduce(jnp.add, [jnp.sum(e, axis=-1, keepdims=True) for e in es])
    if extra is not None:
        l = l + jnp.exp2(extra - m)
    inv = 1.0 / l
    return [e * inv for e in es]


def _na_body(q_ref, k_ref, v_ref, kx_ref, vx_ref, bias_ref, o_ref, *, rows):
    nblk = rows // NA_Q_ROWS
    nq = NA_Q_ROWS * GRID_W
    nkeys = NA_KEY_ROWS * GRID_W
    kx = kx_ref[...]
    vx = vx_ref[...]

    def blk_fn(bi, carry):
        ks = jnp.clip(bi * NA_Q_ROWS - NA_WIN_ROWS // 2, 0, rows - NA_KEY_ROWS)
        cls = jnp.where(bi == 0, 0, jnp.where(bi == nblk - 1, 2, 1))
        qrows = pl.ds(pl.multiple_of(bi * nq, nq), nq)
        krows = pl.ds(pl.multiple_of(ks * GRID_W, GRID_W), nkeys)
        q = q_ref[qrows, :]
        s_loc = _dot_t(q, k_ref[krows, :]) + bias_ref[cls]
        s_ctx = _dot_t(q, kx)
        p_loc, p_ctx = _softmax_parts([s_loc, s_ctx])
        o = _dot(p_loc.astype(BF16), v_ref[krows, :]) + _dot(p_ctx.astype(BF16), vx)
        o_ref[qrows, :] = o.astype(BF16)
        return carry

    lax.fori_loop(0, nblk, blk_fn, 0, unroll=NA_UNROLL)


def _na(qkv, bias_tbl, batch, seq, ctx_len, t):
    rows = seq // GRID_W
    hd = HEAD_DIM
    ctx_blk0 = batch * seq // ctx_len

    def col(off):
        return off // hd

    return pl.pallas_call(
        functools.partial(_na_body, rows=rows),
        grid=(batch, NA_HEADS),
        in_specs=[
            pl.BlockSpec((seq, hd), lambda b, h: (b, col(OFF_QA) + h)),
            pl.BlockSpec((seq, hd), lambda b, h: (b, col(OFF_KA) + h)),
            pl.BlockSpec((seq, hd), lambda b, h: (b, col(OFF_VA) + h)),
            pl.BlockSpec((ctx_len, hd), lambda b, h: (ctx_blk0 + b, col(OFF_KA) + h)),
            pl.BlockSpec((ctx_len, hd), lambda b, h: (ctx_blk0 + b, col(OFF_VA) + h)),
            pl.BlockSpec((None,) + bias_tbl.shape[1:], lambda b, h: (h, 0, 0, 0)),
        ],
        out_specs=pl.BlockSpec((seq, hd), lambda b, h: (b, col(OFF_OA) + h)),
        out_shape=jax.ShapeDtypeStruct((t, ATT_WIDTH), BF16),
        compiler_params=_params(("arbitrary", "arbitrary")),
        name="na_attn",
    )(qkv, qkv, qkv, qkv, qkv, bias_tbl)


def _sw_body(sink_ref, q_ref, k_ref, v_ref, kx_ref, vx_ref, att_in_ref, o_ref, *, seq):
    del att_in_ref
    kv = pl.program_id(1)
    blk = SW_BLOCK
    span = 3 * blk
    g = SW_GROUP
    kx = kx_ref[...]
    vx = vx_ref[...]
    grp = lax.broadcasted_iota(jnp.int32, (g * blk, 1), 0) // blk
    sink_col = jnp.zeros((g * blk, 1), F32)
    for gi in range(g):
        sink_col = jnp.where(grp == gi, sink_ref[kv * g + gi] * LOG2E, sink_col)
    qoff = lax.broadcasted_iota(jnp.int32, (g * blk, span), 0) % blk
    koff = lax.broadcasted_iota(jnp.int32, (g * blk, span), 1)

    def blk_fn(n, carry):
        start = jnp.clip((n - 1) * blk, 0, seq - span)
        qrows = pl.ds(pl.multiple_of(n * blk, blk), blk)
        krows = pl.ds(pl.multiple_of(start, blk), span)
        qb = q_ref[qrows, :]
        qs = jnp.concatenate([qb[:, gi * HEAD_DIM:(gi + 1) * HEAD_DIM] for gi in range(g)], axis=0)
        s_loc = _dot_t(qs, k_ref[krows, :])
        dist = (n * blk + qoff) - (start + koff)
        s_loc = jnp.where(jnp.abs(dist) <= SW_WINDOW, s_loc, NEG_INF)
        s_ctx = _dot_t(qs, kx)
        p_loc, p_ctx = _softmax_parts([s_loc, s_ctx], extra=sink_col)
        o = _dot(p_loc.astype(BF16), v_ref[krows, :]) + _dot(p_ctx.astype(BF16), vx)
        for gi in range(g):
            o_ref[qrows, gi * HEAD_DIM:(gi + 1) * HEAD_DIM] = o[gi * blk:(gi + 1) * blk].astype(BF16)
        return carry

    lax.fori_loop(0, seq // blk, blk_fn, 0, unroll=SW_UNROLL)


def _sw(qkv, att, sink, batch, seq, ctx_len):
    hd = HEAD_DIM
    gw = SW_GROUP * hd
    ctx_blk0 = batch * seq // ctx_len
    return pl.pallas_call(
        functools.partial(_sw_body, seq=seq),
        grid=(batch, SW_KV_HEADS),
        in_specs=[
            pl.BlockSpec(memory_space=pltpu.SMEM),
            pl.BlockSpec((seq, gw), lambda b, k: (b, OFF_QB // gw + k)),
            pl.BlockSpec((seq, hd), lambda b, k: (b, OFF_KB // hd + k)),
            pl.BlockSpec((seq, hd), lambda b, k: (b, OFF_VB // hd + k)),
            pl.BlockSpec((ctx_len, hd), lambda b, k: (ctx_blk0 + b, OFF_KB // hd + k)),
            pl.BlockSpec((ctx_len, hd), lambda b, k: (ctx_blk0 + b, OFF_VB // hd + k)),
            pl.BlockSpec(memory_space=pl.ANY),
        ],
        out_specs=pl.BlockSpec((seq, gw), lambda b, k: (b, OFF_OB // gw + k)),
        out_shape=jax.ShapeDtypeStruct(att.shape, att.dtype),
        input_output_aliases={6: 0},
        compiler_params=_params(("arbitrary", "arbitrary")),
        name="sw_attn",
    )(sink, qkv, qkv, qkv, qkv, qkv, att)


def _df_lambda(lam_ref, lambda_init):
    lv = lam_ref[...]
    s01 = jnp.sum(lv[0:1] * lv[1:2], axis=-1, keepdims=True)
    s23 = jnp.sum(lv[2:3] * lv[3:4], axis=-1, keepdims=True)
    return jnp.exp(s01) - jnp.exp(s23) + lambda_init


def _df_core(q, k_parts, v_parts, lam, subln, lambda_init):
    hd = HEAD_DIM
    exps, weights = [], []
    for c in range(2):
        qc = q[:, c * hd:(c + 1) * hd]
        scores = [_dot_t(qc, kp[:, c * hd:(c + 1) * hd]) for kp in k_parts]
        m = functools.reduce(jnp.maximum, [jnp.max(s, axis=-1, keepdims=True) for s in scores])
        es = [jnp.exp2(s - m) for s in scores]
        l = functools.reduce(jnp.add, [jnp.sum(e, axis=-1, keepdims=True) for e in es])
        exps.append(es)
        weights.append(1.0 / l)
    w0 = weights[0]
    w1 = lam * weights[1]
    o = None
    for i, vp in enumerate(v_parts):
        a = (exps[0][i] * w0 - exps[1][i] * w1).astype(BF16)
        term = _dot(a, vp[...])
        o = term if o is None else o + term
    return o * _rstd(o) * subln * (1.0 - lambda_init)


def _df_body(lam_ref, subln_ref, q_ref, k_ref, v_ref, kx_ref, vx_ref, att_in_ref, o_ref, *,
             lambda_init):
    del att_in_ref
    lam = _df_lambda(lam_ref, lambda_init)
    subln = subln_ref[...]
    for sub in range(DF_Q_STEP // DF_Q_BLOCK):
        rows = slice(sub * DF_Q_BLOCK, (sub + 1) * DF_Q_BLOCK)
        o = _df_core(q_ref[rows, :], [k_ref, kx_ref], [v_ref, vx_ref], lam, subln, lambda_init)
        o_ref[rows, :] = o.astype(BF16)


def _df(qkv, att, lam4, subln4, layer, lambda_init, batch, seq, ctx_len):
    w = 2 * HEAD_DIM
    nqb = seq // DF_Q_STEP
    ctx_blk0 = batch * seq // ctx_len
    return pl.pallas_call(
        functools.partial(_df_body, lambda_init=lambda_init),
        grid=(batch, DF_HEADS, nqb),
        in_specs=[
            pl.BlockSpec((None, 4, HEAD_DIM), lambda b, h, n: (layer, 0, 0)),
            pl.BlockSpec((None, 1, w), lambda b, h, n: (layer, 0, 0)),
            pl.BlockSpec((DF_Q_STEP, w), lambda b, h, n: (b * nqb + n, OFF_QC // w + h)),
            pl.BlockSpec((seq, w), lambda b, h, n: (b, OFF_KC // w + h)),
            pl.BlockSpec((seq, w), lambda b, h, n: (b, OFF_VC // w + h)),
            pl.BlockSpec((ctx_len, w), lambda b, h, n: (ctx_blk0 + b, OFF_KC // w + h)),
            pl.BlockSpec((ctx_len, w), lambda b, h, n: (ctx_blk0 + b, OFF_VC // w + h)),
            pl.BlockSpec(memory_space=pl.ANY),
        ],
        out_specs=pl.BlockSpec((DF_Q_STEP, w), lambda b, h, n: (b * nqb + n, OFF_OC // w + h)),
        out_shape=jax.ShapeDtypeStruct(att.shape, att.dtype),
        input_output_aliases={7: 0},
        compiler_params=_params(("arbitrary", "arbitrary", "arbitrary")),
        name="df_attn",
    )(lam4, subln4, qkv, qkv, qkv, qkv, qkv, att)


def _ctx_body(sink_ref, lam_ref, subln_ref, x_ref, att_in_ref, o_ref, *, lambda_init):
    del att_in_ref
    hd = HEAD_DIM

    def cols(off, width=hd):
        return x_ref[:, off:off + width]

    for h in range(NA_HEADS):
        s = _dot_t(cols(OFF_QA + h * hd), cols(OFF_KA + h * hd))
        (p,) = _softmax_parts([s])
        o = _dot(p.astype(BF16), cols(OFF_VA + h * hd))
        o_ref[:, OFF_OA + h * hd:OFF_OA + (h + 1) * hd] = o.astype(BF16)

    for hq in range(SW_HEADS):
        kvh = hq // SW_GROUP
        s = _dot_t(cols(OFF_QB + hq * hd), cols(OFF_KB + kvh * hd))
        sink = jnp.full((s.shape[0], 1), sink_ref[hq] * LOG2E, F32)
        (p,) = _softmax_parts([s], extra=sink)
        o = _dot(p.astype(BF16), cols(OFF_VB + kvh * hd))
        o_ref[:, OFF_OB + hq * hd:OFF_OB + (hq + 1) * hd] = o.astype(BF16)

    lam = _df_lambda(lam_ref, lambda_init)
    subln = subln_ref[...]
    for h in range(DF_HEADS):
        w = 2 * hd
        o = _df_core(cols(OFF_QC + h * w, w), [cols(OFF_KC + h * w, w)], [cols(OFF_VC + h * w, w)],
                     lam, subln, lambda_init)
        o_ref[:, OFF_OC + h * w:OFF_OC + (h + 1) * w] = o.astype(BF16)


def _ctx_attn(qkv, att, sink, lam4, subln4, layer, lambda_init, batch, seq, ctx_len):
    ctx_blk0 = batch * seq // ctx_len
    return pl.pallas_call(
        functools.partial(_ctx_body, lambda_init=lambda_init),
        grid=(batch,),
        in_specs=[
            pl.BlockSpec(memory_space=pltpu.SMEM),
            pl.BlockSpec((None, 4, HEAD_DIM), lambda b: (layer, 0, 0)),
            pl.BlockSpec((None, 1, 2 * HEAD_DIM), lambda b: (layer, 0, 0)),
            pl.BlockSpec((ctx_len, IN_WIDTH), lambda b: (ctx_blk0 + b, 0)),
            pl.BlockSpec(memory_space=pl.ANY),
        ],
        out_specs=pl.BlockSpec((ctx_len, ATT_WIDTH), lambda b: (ctx_blk0 + b, 0)),
        out_shape=jax.ShapeDtypeStruct(att.shape, att.dtype),
        input_output_aliases={4: 0},
        compiler_params=_params(("arbitrary",)),
        name="ctx_attn",
    )(sink, lam4, subln4, qkv, att)


def _merge_body(h_ref, att_ref, wg_ref, bg_ref, wb_ref, o_ref, *, branch_width):
    h = h_ref[...]
    acc = None
    for i in range(3):
        gate = jax.nn.sigmoid(_dot(h, wg_ref[i]) + bg_ref[i])
        br = _dot(att_ref[:, i * branch_width:(i + 1) * branch_width], wb_ref[i])
        acc = gate * br if acc is None else acc + gate * br
    o_ref[...] = acc.astype(BF16)


def _merge(h, att, w_gate, b_gate4, w_branch, layer, rows):
    d = h.shape[1]
    nb, bw = w_branch.shape[1], w_branch.shape[2]
    tm, tn = TM_MERGE, TN_MERGE
    return pl.pallas_call(
        functools.partial(_merge_body, branch_width=bw),
        grid=(rows // tm, d // tn),
        in_specs=[
            pl.BlockSpec((tm, d), lambda i, j: (i, 0)),
            pl.BlockSpec((tm, nb * bw), lambda i, j: (i, 0)),
            pl.BlockSpec((None, nb, d, tn), lambda i, j: (layer, 0, 0, j)),
            pl.BlockSpec((None, nb, 1, tn), lambda i, j: (layer, 0, 0, j)),
            pl.BlockSpec((None, nb, bw, tn), lambda i, j: (layer, 0, 0, j)),
        ],
        out_specs=pl.BlockSpec((tm, tn), lambda i, j: (i, j)),
        out_shape=jax.ShapeDtypeStruct((rows, d), BF16),
        compiler_params=_params(("arbitrary", "arbitrary")),
        name="merge",
    )(h, att, w_gate, b_gate4, w_branch)


@jax.jit
def _forward(x, c, ctx, c_ctx, w_ada, b_ada, norm_g, w_ffn_gate, w_ffn_up, w_ffn_down,
             w_in, na_rpb, sw_sink, df_lambda, df_subln_g, w_branch, w_gate, b_gate, w_out):
    batch, seq, d = x.shape
    ctx_len = ctx.shape[1]
    depth = w_ada.shape[0]
    t_lat = batch * seq
    t = t_lat + batch * ctx_len
    assert seq % TM_UP == 0 and (batch * ctx_len) % TM_UP == 0 and batch + 1 <= MOD_ROWS
    assert seq % ctx_len == 0 and ctx_len % SW_BLOCK == 0

    xs = jnp.concatenate([x.reshape(t_lat, d), ctx.reshape(batch * ctx_len, d)], axis=0)
    cc = jnp.zeros((MOD_ROWS, d), F32).at[:batch].set(c).at[batch].set(c_ctx)
    mod4 = _adaln(cc, w_ada, b_ada).reshape(depth, MOD_ROWS, 1, N_MOD * d)
    gains4 = norm_g.reshape(depth, norm_g.shape[1], 1, d)

    wg16 = w_ffn_gate.astype(BF16)
    wu16 = w_ffn_up.astype(BF16)
    wd16 = w_ffn_down.astype(BF16)
    win16 = w_in.astype(BF16)
    wbr16 = w_branch.astype(BF16)
    wgt16 = w_gate.astype(BF16)
    wout16 = w_out.astype(BF16)
    b_gate4 = b_gate.reshape(depth, b_gate.shape[1], 1, d)
    lam4 = df_lambda.astype(F32)
    subln4 = df_subln_g.reshape(depth, 1, -1)
    cos_t, sin_t = _rope_tables(seq)

    h = _norm_mod(xs, mod4, gains4, 0, seq)
    for l in range(depth):
        last = l == depth - 1
        lambda_init = 0.8 - 0.6 * math.exp(-0.3 * l)
        u = _ffn_up(h, wg16, wu16, l, 0)
        xs, h = _down_epi(u, wd16, (l, 0), xs, t, seq, mod4, gains4, l, 2, 1, 0.5,
                          nxt=(mod4, gains4, l, 2, 3, 4))
        qkv = _proj(h, win16, l, cos_t, sin_t, seq, t_lat)
        att = _na(qkv, _na_bias_table(na_rpb[l], seq // GRID_W), batch, seq, ctx_len, t)
        att = _sw(qkv, att, sw_sink[l], batch, seq, ctx_len)
        att = _df(qkv, att, lam4, subln4, l, lambda_init, batch, seq, ctx_len)
        if not last:
            att = _ctx_attn(qkv, att, sw_sink[l], lam4, subln4, l, lambda_init, batch, seq, ctx_len)
        rows = t_lat if last else t
        merged = _merge(h, att, wgt16, b_gate4, wbr16, l, rows)
        xs, h = _down_epi(merged, wout16, (l,), xs, rows, seq, mod4, gains4, l, 5, 3, 1.0,
                          nxt=(mod4, gains4, l, 4, 6, 7))
        u = _ffn_up(h, wg16, wu16, l, 1)
        nxt = None if last else (mod4, gains4, l + 1, 0, 0, 1)
        xs, h = _down_epi(u, wd16, (l, 1), xs, rows, seq, mod4, gains4, l, 8, 5, 0.5, nxt=nxt)
    return xs.reshape(batch, seq, d)


def kernel(x, c, ctx, c_ctx, w_ada, b_ada, norm_g, w_ffn_gate, w_ffn_up, w_ffn_down,
           w_in, na_rpb, sw_sink, df_lambda, df_subln_g, w_branch, w_gate, b_gate, w_out):
    return _forward(x, c, ctx, c_ctx, w_ada, b_ada, norm_g, w_ffn_gate, w_ffn_up, w_ffn_down,
                    w_in, na_rpb, sw_sink, df_lambda, df_subln_g, w_branch, w_gate, b_gate, w_out)
```

```python
import functools
import math

import numpy as np
import jax
import jax.numpy as jnp
from jax import lax
from jax.experimental import pallas as pl
from jax.experimental.pallas import tpu as pltpu

F32 = jnp.float32
BF16 = jnp.bfloat16

GRID_W = 64
HEAD_DIM = 128
NA_HEADS = 8
NA_WIN_ROWS = 8
NA_WIN_COLS = 16
SW_HEADS = 8
SW_KV_HEADS = 2
SW_GROUP = SW_HEADS // SW_KV_HEADS
SW_WINDOW = 128
SW_BLOCK = 128
DF_HEADS = 4
DF_Q_BLOCK = 128
N_MOD = 9
ROPE_BASE = 10000.0
EPS = 1e-6
NEG_INF = -1e30
LOG2E = math.log2(math.e)
Q_SCALE = HEAD_DIM ** -0.5 * LOG2E

OFF_QA, OFF_KA, OFF_VA = 0, 1024, 2048
OFF_QB, OFF_KB, OFF_VB = 3072, 4096, 4352
OFF_QC, OFF_KC, OFF_VC = 4608, 5632, 6656
IN_WIDTH = 7680
OFF_OA, OFF_OB, OFF_OC = 0, 1024, 2048
ATT_WIDTH = 3072

V7X_LANES = 128
MOD_ROWS = 8
V7X_VMEM_LIMIT = 58 * 1024 * 1024

TM_UP, TN_UP = 1024, 512
TM_DOWN, TK_DOWN = 1024, 512
TM_PROJ, TN_PROJ = 1024, 512
DF_Q_STEP = 2 * DF_Q_BLOCK
NA_Q_ROWS = 4
NA_KEY_ROWS = 12
NA_UNROLL = 2
SW_UNROLL = 2
TM_MERGE, TN_MERGE = 1024, 256
TN_ADA = 512
EPI_ROWS = 16
EPI_UNROLL = 8


def _params(sem):
    return pltpu.CompilerParams(dimension_semantics=sem, vmem_limit_bytes=V7X_VMEM_LIMIT)


def _dot(a, b):
    return jnp.dot(a, b, preferred_element_type=F32)


def _dot_t(a, b):
    return lax.dot_general(a, b, (((1,), (1,)), ((), ())), preferred_element_type=F32)


def _rstd(y):
    return lax.rsqrt(jnp.mean(y * y, axis=-1, keepdims=True) + EPS)


def _silu(a):
    return a * jax.nn.sigmoid(a)


def _adaln_body(c_ref, w_ref, b_ref, o_ref):
    s = _silu(c_ref[...]).astype(BF16)
    o_ref[...] = _dot(s, w_ref[...].astype(BF16)) + b_ref[...]


def _adaln(cc, w_ada, b_ada):
    depth, d, nd = w_ada.shape
    return pl.pallas_call(
        _adaln_body,
        grid=(depth, nd // TN_ADA),
        in_specs=[
            pl.BlockSpec((MOD_ROWS, d), lambda l, j: (0, 0)),
            pl.BlockSpec((None, d, TN_ADA), lambda l, j: (l, 0, j)),
            pl.BlockSpec((None, 1, TN_ADA), lambda l, j: (l, 0, j)),
        ],
        out_specs=pl.BlockSpec((None, MOD_ROWS, TN_ADA), lambda l, j: (l, 0, j)),
        out_shape=jax.ShapeDtypeStruct((depth, MOD_ROWS, nd), F32),
        compiler_params=_params(("arbitrary", "arbitrary")),
        name="adaln",
    )(cc, w_ada, b_ada.reshape(depth, 1, nd))


def _mod_spec(layer, chunk, d, tm, seq):
    return pl.BlockSpec((None, None, 1, d), lambda i, *_: (layer, (i * tm) // seq, 0, chunk))


def _gain_spec(layer, idx, d):
    return pl.BlockSpec((None, None, 1, d), lambda i, *_: (layer, idx, 0, 0))


def _norm_mod_body(x_ref, g_ref, shift_ref, scale_ref, o_ref):
    x = x_ref[...]
    y = x * _rstd(x) * g_ref[...]
    o_ref[...] = (y * (1.0 + scale_ref[...]) + shift_ref[...]).astype(BF16)


def _norm_mod(xs, mod4, gains4, layer, seq):
    t, d = xs.shape
    tm = 256
    return pl.pallas_call(
        _norm_mod_body,
        grid=(t // tm,),
        in_specs=[
            pl.BlockSpec((tm, d), lambda i: (i, 0)),
            _gain_spec(layer, 0, d),
            _mod_spec(layer, 0, d, tm, seq),
            _mod_spec(layer, 1, d, tm, seq),
        ],
        out_specs=pl.BlockSpec((tm, d), lambda i: (i, 0)),
        out_shape=jax.ShapeDtypeStruct((t, d), BF16),
        compiler_params=_params(("arbitrary",)),
        name="norm_mod",
    )(xs, gains4, mod4, mod4)


def _ffn_up_body(h_ref, wg_ref, wu_ref, o_ref):
    h = h_ref[...]
    a = _dot(h, wg_ref[...])
    b = _dot(h, wu_ref[...])
    o_ref[...] = (_silu(a) * b).astype(BF16)


def _col_tiled(w, tn):
    *lead, k, n = w.shape
    nl = len(lead)
    w = w.astype(BF16).reshape(*lead, k, n // tn, tn)
    return w.transpose(*range(nl), nl + 1, nl, nl + 2)


def _ffn_up(h, w_gate, w_up, layer, which):
    t, d = h.shape
    n_tiles = w_gate.shape[2]
    wspec = pl.BlockSpec((None, None, None, d, TN_UP), lambda i, j: (layer, which, j, 0, 0))
    return pl.pallas_call(
        _ffn_up_body,
        grid=(t // TM_UP, n_tiles),
        in_specs=[pl.BlockSpec((TM_UP, d), lambda i, j: (i, 0)), wspec, wspec],
        out_specs=pl.BlockSpec((None, TM_UP, TN_UP), lambda i, j: (j, i, 0)),
        out_shape=jax.ShapeDtypeStruct((n_tiles, t, TN_UP), BF16),
        compiler_params=_params(("arbitrary", "arbitrary")),
        name="ffn_up",
    )(h, w_gate, w_up)


def _down_epi_body(a_ref, w_ref, x_hbm, gate_ref, gpost_ref, *rest, coef, nk, tm, with_next):
    if with_next:
        (gnext_ref, shift_ref, scale_ref, xo_hbm, ho_hbm,
         acc_ref, xbuf_ref, stat_ref, sem_ref, hbuf_ref) = rest
    else:
        xo_hbm, acc_ref, xbuf_ref, stat_ref, sem_ref = rest
    i = pl.program_id(0)
    k = pl.program_id(1)
    n_tiles = pl.num_programs(0)
    d = acc_ref.shape[-1]
    sem_in, sem_xout, sem_hout = 0, 1, 2

    def tile_rows(tile):
        return pl.ds(pl.multiple_of(tile * tm, tm), tm)

    def residual_read(tile):
        return pltpu.make_async_copy(x_hbm.at[tile_rows(tile), :], xbuf_ref, sem_ref.at[sem_in])

    def x_writeback(tile):
        return pltpu.make_async_copy(xbuf_ref, xo_hbm.at[tile_rows(tile), :], sem_ref.at[sem_xout])

    def h_writeback(tile):
        return pltpu.make_async_copy(hbuf_ref, ho_hbm.at[tile_rows(tile), :], sem_ref.at[sem_hout])

    @pl.when(k == 0)
    def _():
        acc_ref[...] = _dot(a_ref[...], w_ref[...])

    @pl.when(k != 0)
    def _():
        acc_ref[...] += _dot(a_ref[...], w_ref[...])

    @pl.when(k == 1)
    def _():
        @pl.when(i > 0)
        def _():
            x_writeback(i - 1).wait()

        residual_read(i).start()

    @pl.when(k == nk - 1)
    def _():
        gate_gain = (coef * gate_ref[...]) * gpost_ref[...]
        if with_next:
            next_gain = gnext_ref[...] * (1.0 + scale_ref[...])
            shift = shift_ref[...]

        lanes = stat_ref.shape[-1]

        def chunk_rows(c):
            return pl.ds(pl.multiple_of(c * EPI_ROWS, EPI_ROWS), EPI_ROWS)

        def put_stat(slot, rows, val):
            stat_ref[slot, rows, :] = jnp.broadcast_to(val, (EPI_ROWS, lanes))

        def get_stat(slot, rows):
            return jnp.tile(stat_ref[slot, rows, :], (1, d // lanes))

        def pass_norm(c, carry):
            rows = chunk_rows(c)
            put_stat(0, rows, _rstd(acc_ref[rows, :]))
            return carry

        def pass_residual(c, carry):
            rows = chunk_rows(c)
            xn = xbuf_ref[rows, :] + (acc_ref[rows, :] * get_stat(0, rows)) * gate_gain
            xbuf_ref[rows, :] = xn
            if with_next:
                put_stat(1, rows, _rstd(xn))
            return carry

        def pass_modulate(c, carry):
            rows = chunk_rows(c)
            hn = (xbuf_ref[rows, :] * get_stat(1, rows)) * next_gain + shift
            hbuf_ref[rows, :] = hn.astype(BF16)
            return carry

        n_chunks = tm // EPI_ROWS
        lax.fori_loop(0, n_chunks, pass_norm, 0, unroll=EPI_UNROLL)
        residual_read(i).wait()
        lax.fori_loop(0, n_chunks, pass_residual, 0, unroll=EPI_UNROLL)
        x_writeback(i).start()
        if with_next:
            @pl.when(i > 0)
            def _():
                h_writeback(i - 1).wait()

            lax.fori_loop(0, n_chunks, pass_modulate, 0, unroll=EPI_UNROLL)
            h_writeback(i).start()

        @pl.when(i == n_tiles - 1)
        def _():
            x_writeback(i).wait()
            if with_next:
                h_writeback(i).wait()


def _down_epi(a, w, wsel, xs, rows, seq, mod4, gains4, layer, gate_chunk, gpost_idx, coef,
              nxt=None):
    d = xs.shape[1]
    tm, tk = TM_DOWN, TK_DOWN
    if a.ndim == 3:
        assert a.shape[2] == tk
        nk = a.shape[0]
        aspec = pl.BlockSpec((None, tm, tk), lambda i, k: (k, i, 0))
    else:
        nk = a.shape[1] // tk
        aspec = pl.BlockSpec((tm, tk), lambda i, k: (i, k))
    assert nk >= 3 and rows % tm == 0
    nlead = len(wsel)
    wspec = pl.BlockSpec((None,) * nlead + (tk, d), lambda i, k: tuple(wsel) + (k, 0))
    hbm = pl.BlockSpec(memory_space=pl.ANY)
    in_specs = [
        aspec,
        wspec,
        hbm,
        _mod_spec(layer, gate_chunk, d, tm, seq),
        _gain_spec(layer, gpost_idx, d),
    ]
    args = [a, w, xs, mod4, gains4]
    out_specs = [hbm]
    out_shape = [jax.ShapeDtypeStruct((rows, d), F32)]
    scratch = [
        pltpu.VMEM((tm, d), F32),
        pltpu.VMEM((tm, d), F32),
        pltpu.VMEM((2, tm, V7X_LANES), F32),
        pltpu.SemaphoreType.DMA((3,)),
    ]
    if nxt is not None:
        mod_n, gains_n, layer_n, gain_idx, shift_chunk, scale_chunk = nxt
        in_specs += [
            _gain_spec(layer_n, gain_idx, d),
            _mod_spec(layer_n, shift_chunk, d, tm, seq),
            _mod_spec(layer_n, scale_chunk, d, tm, seq),
        ]
        args += [gains_n, mod_n, mod_n]
        out_specs.append(hbm)
        out_shape.append(jax.ShapeDtypeStruct((rows, d), BF16))
        scratch.append(pltpu.VMEM((tm, d), BF16))
    body = functools.partial(_down_epi_body, coef=coef, nk=nk, tm=tm, with_next=nxt is not None)
    out = pl.pallas_call(
        body,
        grid=(rows // tm, nk),
        in_specs=in_specs,
        out_specs=out_specs,
        out_shape=out_shape,
        scratch_shapes=scratch,
        compiler_params=_params(("arbitrary", "arbitrary")),
        name="down_epi",
    )(*args)
    return (out[0], out[1]) if nxt is not None else (out[0], None)


def _in_tiles(j, lo, hi):
    return jnp.logical_and(j >= lo // TN_PROJ, j < hi // TN_PROJ)


def _proj_body(h_ref, w_ref, cos_ref, sin_ref, o_ref, *, n_lat_tiles):
    i = pl.program_id(0)
    j = pl.program_id(1)
    hd = HEAD_DIM
    acc = _dot(h_ref[...], w_ref[...])
    is_q = _in_tiles(j, OFF_QA, OFF_KA) | _in_tiles(j, OFF_QB, OFF_KB) | _in_tiles(j, OFF_QC, OFF_KC)
    qscale = jnp.where(is_q, Q_SCALE, 1.0).astype(F32)
    latent = i < n_lat_tiles
    rope_full = latent & (_in_tiles(j, OFF_QB, OFF_KB) | _in_tiles(j, OFF_QC, OFF_VC))
    rope_half = latent & _in_tiles(j, OFF_KB, OFF_QC)

    def rope_heads(n_heads, scale=None):
        cos = cos_ref[...]
        sin = sin_ref[...]
        if scale is not None:
            cos = cos * scale
            sin = sin * scale
        lane = lax.broadcasted_iota(jnp.int32, cos.shape, 1)
        quarter = hd // 4
        first = (lane % (2 * quarter)) < quarter
        for hh in range(n_heads):
            a = acc[:, hh * hd:(hh + 1) * hd]
            partner = jnp.where(first, pltpu.roll(a, hd - quarter, 1), pltpu.roll(a, quarter, 1))
            o_ref[:, hh * hd:(hh + 1) * hd] = (a * cos + partner * sin).astype(BF16)

    @pl.when(rope_full)
    def _():
        rope_heads(TN_PROJ // hd, qscale)

    @pl.when(rope_half)
    def _():
        n_rope = (OFF_VB - OFF_KB) // hd
        rope_heads(n_rope)
        o_ref[:, n_rope * hd:] = acc[:, n_rope * hd:].astype(BF16)

    @pl.when(jnp.logical_not(rope_full | rope_half))
    def _():
        o_ref[...] = (acc * qscale).astype(BF16)


def _proj(h, w_in, layer, cos_t, sin_t, seq, t_lat):
    t, d = h.shape
    n_tiles = w_in.shape[1]
    width = n_tiles * TN_PROJ
    pos_tiles = seq // TM_PROJ
    assert OFF_KB % TN_PROJ == 0 and OFF_QC - OFF_KB == TN_PROJ
    tspec = pl.BlockSpec((TM_PROJ, HEAD_DIM), lambda i, j: (i % pos_tiles, 0))
    return pl.pallas_call(
        functools.partial(_proj_body, n_lat_tiles=t_lat // TM_PROJ),
        grid=(t // TM_PROJ, n_tiles),
        in_specs=[
            pl.BlockSpec((TM_PROJ, d), lambda i, j: (i, 0)),
            pl.BlockSpec((None, None, d, TN_PROJ), lambda i, j: (layer, j, 0, 0)),
            tspec, tspec,
        ],
        out_specs=pl.BlockSpec((TM_PROJ, TN_PROJ), lambda i, j: (i, j)),
        out_shape=jax.ShapeDtypeStruct((t, width), BF16),
        compiler_params=_params(("arbitrary", "arbitrary")),
        name="proj_rope",
    )(h, w_in, cos_t, sin_t)


def _rope_tables(n):
    t = jnp.arange(n, dtype=jnp.int32)
    row = (t // GRID_W).astype(F32)
    col = (t % GRID_W).astype(F32)
    half = HEAD_DIM // 2
    inv = ROPE_BASE ** (-jnp.arange(0, half, 2, dtype=F32) / half)
    ang_r = row[:, None] * inv
    ang_c = col[:, None] * inv
    cos_h = jnp.concatenate([jnp.cos(ang_r)] * 2 + [jnp.cos(ang_c)] * 2, axis=-1)
    sin_h = jnp.concatenate([-jnp.sin(ang_r), jnp.sin(ang_r), -jnp.sin(ang_c), jnp.sin(ang_c)], axis=-1)
    return cos_h, sin_h


def _na_key_row_start(blk, rows):
    return np.clip(blk * NA_Q_ROWS - NA_WIN_ROWS // 2, 0, rows - NA_KEY_ROWS)


def _na_bias_table(rpb, rows):
    kh, kw = NA_WIN_ROWS, NA_WIN_COLS
    nblk = rows // NA_Q_ROWS
    assert rows % NA_Q_ROWS == 0 and nblk >= 3 and rows >= NA_KEY_ROWS
    qc = np.arange(GRID_W)
    kc = np.arange(GRID_W)
    col_start = np.clip(qc - kw // 2, 0, GRID_W - kw)
    col_ok = (kc[None, :] >= col_start[:, None]) & (kc[None, :] < col_start[:, None] + kw)
    dc = kc[None, :] - qc[:, None] + NA_WIN_COLS - 1
    col_hot = (dc[None] == np.arange(2 * NA_WIN_COLS - 1)[:, None, None]).astype(np.float32)
    n_dr = 2 * NA_WIN_ROWS - 1
    row_hot = np.zeros((3, NA_Q_ROWS, NA_KEY_ROWS, n_dr), np.float32)
    for cls, blk in enumerate((0, 1, nblk - 1)):
        ks = _na_key_row_start(blk, rows)
        for j in range(NA_Q_ROWS):
            r = blk * NA_Q_ROWS + j
            rs = np.clip(r - kh // 2, 0, rows - kh)
            for i in range(NA_KEY_ROWS):
                if rs <= ks + i < rs + kh:
                    row_hot[cls, j, i, ks + i - r + NA_WIN_ROWS - 1] = 1.0
    b = jnp.einsum('cjir,hrd,dqk->hcjqik', row_hot, rpb.astype(F32) * LOG2E, col_hot,
                   precision=lax.Precision.HIGHEST)
    ok = (row_hot.sum(-1) > 0)[:, :, None, :, None] & col_ok[None, None, :, None, :]
    b = jnp.where(ok[None], b, NEG_INF)
    return b.reshape(rpb.shape[0], 3, NA_Q_ROWS * GRID_W, NA_KEY_ROWS * GRID_W)


def _softmax_parts(parts, extra=None):
    m = functools.reduce(jnp.maximum, [jnp.max(p, axis=-1, keepdims=True) for p in parts])
    if extra is not None:
        m = jnp.maximum(m, extra)
    es = [jnp.exp2(p - m) for p in parts]
    l = functools.reduce(jnp.add, [jnp.sum(e, axis=-1, keepdims=True) for e in es])
    if extra is not None:
        l = l + jnp.exp2(extra - m)
    inv = 1.0 / l
    return [e * inv for e in es]


def _na_body(q_ref, k_ref, v_ref, kx_ref, vx_ref, bias_ref, att_in_ref, o_ref, *, rows):
    del att_in_ref
    nblk = rows // NA_Q_ROWS
    nq = NA_Q_ROWS * GRID_W
    nkeys = NA_KEY_ROWS * GRID_W
    kx = kx_ref[...]
    vx = vx_ref[...]

    def blk_fn(bi, carry):
        ks = jnp.clip(bi * NA_Q_ROWS - NA_WIN_ROWS // 2, 0, rows - NA_KEY_ROWS)
        cls = jnp.where(bi == 0, 0, jnp.where(bi == nblk - 1, 2, 1))
        qrows = pl.ds(pl.multiple_of(bi * nq, nq), nq)
        krows = pl.ds(pl.multiple_of(ks * GRID_W, GRID_W), nkeys)
        q = q_ref[qrows, :]
        s_loc = _dot_t(q, k_ref[krows, :]) + bias_ref[cls]
        s_ctx = _dot_t(q, kx)
        p_loc, p_ctx = _softmax_parts([s_loc, s_ctx])
        o = _dot(p_loc.astype(BF16), v_ref[krows, :]) + _dot(p_ctx.astype(BF16), vx)
        o_ref[qrows, :] = o.astype(BF16)
        return carry

    lax.fori_loop(0, nblk, blk_fn, 0, unroll=NA_UNROLL)


def _na(qkv, att, bias_tbl, batch, seq, ctx_len):
    rows = seq // GRID_W
    hd = HEAD_DIM
    ctx_blk0 = batch * seq // ctx_len

    def col(off):
        return off // hd

    return pl.pallas_call(
        functools.partial(_na_body, rows=rows),
        grid=(batch, NA_HEADS),
        in_specs=[
            pl.BlockSpec((seq, hd), lambda b, h: (b, col(OFF_QA) + h)),
            pl.BlockSpec((seq, hd), lambda b, h: (b, col(OFF_KA) + h)),
            pl.BlockSpec((seq, hd), lambda b, h: (b, col(OFF_VA) + h)),
            pl.BlockSpec((ctx_len, hd), lambda b, h: (ctx_blk0 + b, col(OFF_KA) + h)),
            pl.BlockSpec((ctx_len, hd), lambda b, h: (ctx_blk0 + b, col(OFF_VA) + h)),
            pl.BlockSpec((None,) + bias_tbl.shape[1:], lambda b, h: (h, 0, 0, 0)),
            pl.BlockSpec(memory_space=pl.ANY),
        ],
        out_specs=pl.BlockSpec((seq, hd), lambda b, h: (b, col(OFF_OA) + h)),
        out_shape=jax.ShapeDtypeStruct(att.shape, att.dtype),
        input_output_aliases={6: 0},
        compiler_params=_params(("arbitrary", "arbitrary")),
        name="na_attn",
    )(qkv, qkv, qkv, qkv, qkv, bias_tbl, att)


def _sw_body(sink_ref, q_ref, k_ref, v_ref, kx_ref, vx_ref, att_in_ref, o_ref, *, seq):
    del att_in_ref
    kv = pl.program_id(1)
    blk = SW_BLOCK
    span = 3 * blk
    g = SW_GROUP
    kx = kx_ref[...]
    vx = vx_ref[...]
    grp = lax.broadcasted_iota(jnp.int32, (g * blk, 1), 0) // blk
    sink_col = jnp.zeros((g * blk, 1), F32)
    for gi in range(g):
        sink_col = jnp.where(grp == gi, sink_ref[kv * g + gi] * LOG2E, sink_col)
    qoff = lax.broadcasted_iota(jnp.int32, (g * blk, span), 0) % blk
    koff = lax.broadcasted_iota(jnp.int32, (g * blk, span), 1)

    def blk_fn(n, carry):
        start = jnp.clip((n - 1) * blk, 0, seq - span)
        qrows = pl.ds(pl.multiple_of(n * blk, blk), blk)
        krows = pl.ds(pl.multiple_of(start, blk), span)
        qb = q_ref[qrows, :]
        qs = jnp.concatenate([qb[:, gi * HEAD_DIM:(gi + 1) * HEAD_DIM] for gi in range(g)], axis=0)
        s_loc = _dot_t(qs, k_ref[krows, :])
        dist = (n * blk + qoff) - (start + koff)
        s_loc = jnp.where(jnp.abs(dist) <= SW_WINDOW, s_loc, NEG_INF)
        s_ctx = _dot_t(qs, kx)
        p_loc, p_ctx = _softmax_parts([s_loc, s_ctx], extra=sink_col)
        o = _dot(p_loc.astype(BF16), v_ref[krows, :]) + _dot(p_ctx.astype(BF16), vx)
        for gi in range(g):
            o_ref[qrows, gi * HEAD_DIM:(gi + 1) * HEAD_DIM] = o[gi * blk:(gi + 1) * blk].astype(BF16)
        return carry

    lax.fori_loop(0, seq // blk, blk_fn, 0, unroll=SW_UNROLL)


def _sw(qkv, att, sink, batch, seq, ctx_len):
    hd = HEAD_DIM
    gw = SW_GROUP * hd
    ctx_blk0 = batch * seq // ctx_len
    return pl.pallas_call(
        functools.partial(_sw_body, seq=seq),
        grid=(batch, SW_KV_HEADS),
        in_specs=[
            pl.BlockSpec(memory_space=pltpu.SMEM),
            pl.BlockSpec((seq, gw), lambda b, k: (b, OFF_QB // gw + k)),
            pl.BlockSpec((seq, hd), lambda b, k: (b, OFF_KB // hd + k)),
            pl.BlockSpec((seq, hd), lambda b, k: (b, OFF_VB // hd + k)),
            pl.BlockSpec((ctx_len, hd), lambda b, k: (ctx_blk0 + b, OFF_KB // hd + k)),
            pl.BlockSpec((ctx_len, hd), lambda b, k: (ctx_blk0 + b, OFF_VB // hd + k)),
            pl.BlockSpec(memory_space=pl.ANY),
        ],
        out_specs=pl.BlockSpec((seq, gw), lambda b, k: (b, OFF_OB // gw + k)),
        out_shape=jax.ShapeDtypeStruct(att.shape, att.dtype),
        input_output_aliases={6: 0},
        compiler_params=_params(("arbitrary", "arbitrary")),
        name="sw_attn",
    )(sink, qkv, qkv, qkv, qkv, qkv, att)


def _df_lambda(lam_ref, lambda_init):
    lv = lam_ref[...]
    s01 = jnp.sum(lv[0:1] * lv[1:2], axis=-1, keepdims=True)
    s23 = jnp.sum(lv[2:3] * lv[3:4], axis=-1, keepdims=True)
    return jnp.exp(s01) - jnp.exp(s23) + lambda_init


def _df_core(q, k_parts, v_parts, lam, subln, lambda_init):
    hd = HEAD_DIM
    exps, weights = [], []
    for c in range(2):
        qc = q[:, c * hd:(c + 1) * hd]
        scores = [_dot_t(qc, kp[:, c * hd:(c + 1) * hd]) for kp in k_parts]
        m = functools.reduce(jnp.maximum, [jnp.max(s, axis=-1, keepdims=True) for s in scores])
        es = [jnp.exp2(s - m) for s in scores]
        l = functools.reduce(jnp.add, [jnp.sum(e, axis=-1, keepdims=True) for e in es])
        exps.append(es)
        weights.append(1.0 / l)
    w0 = weights[0]
    w1 = lam * weights[1]
    o = None
    for i, vp in enumerate(v_parts):
        a = (exps[0][i] * w0 - exps[1][i] * w1).astype(BF16)
        term = _dot(a, vp[...])
        o = term if o is None else o + term
    return o * _rstd(o) * subln * (1.0 - lambda_init)


def _df_body(lam_ref, subln_ref, q_ref, k_ref, v_ref, kx_ref, vx_ref, att_in_ref, o_ref, *,
             lambda_init):
    del att_in_ref
    lam = _df_lambda(lam_ref, lambda_init)
    subln = subln_ref[...]
    for sub in range(DF_Q_STEP // DF_Q_BLOCK):
        rows = slice(sub * DF_Q_BLOCK, (sub + 1) * DF_Q_BLOCK)
        o = _df_core(q_ref[rows, :], [k_ref, kx_ref], [v_ref, vx_ref], lam, subln, lambda_init)
        o_ref[rows, :] = o.astype(BF16)


def _df(qkv, att, lam4, subln4, layer, lambda_init, batch, seq, ctx_len):
    w = 2 * HEAD_DIM
    nqb = seq // DF_Q_STEP
    ctx_blk0 = batch * seq // ctx_len
    return pl.pallas_call(
        functools.partial(_df_body, lambda_init=lambda_init),
        grid=(batch, DF_HEADS, nqb),
        in_specs=[
            pl.BlockSpec((None, 4, HEAD_DIM), lambda b, h, n: (layer, 0, 0)),
            pl.BlockSpec((None, 1, w), lambda b, h, n: (layer, 0, 0)),
            pl.BlockSpec((DF_Q_STEP, w), lambda b, h, n: (b * nqb + n, OFF_QC // w + h)),
            pl.BlockSpec((seq, w), lambda b, h, n: (b, OFF_KC // w + h)),
            pl.BlockSpec((seq, w), lambda b, h, n: (b, OFF_VC // w + h)),
            pl.BlockSpec((ctx_len, w), lambda b, h, n: (ctx_blk0 + b, OFF_KC // w + h)),
            pl.BlockSpec((ctx_len, w), lambda b, h, n: (ctx_blk0 + b, OFF_VC // w + h)),
            pl.BlockSpec(memory_space=pl.ANY),
        ],
        out_specs=pl.BlockSpec((DF_Q_STEP, w), lambda b, h, n: (b * nqb + n, OFF_OC // w + h)),
        out_shape=jax.ShapeDtypeStruct(att.shape, att.dtype),
        input_output_aliases={7: 0},
        compiler_params=_params(("arbitrary", "arbitrary", "arbitrary")),
        name="df_attn",
    )(lam4, subln4, qkv, qkv, qkv, qkv, qkv, att)


def _ctx_body(sink_ref, lam_ref, subln_ref, x_ref, att_in_ref, o_ref, *, lambda_init):
    del att_in_ref
    hd = HEAD_DIM

    def cols(off, width=hd):
        return x_ref[:, off:off + width]

    for h in range(NA_HEADS):
        s = _dot_t(cols(OFF_QA + h * hd), cols(OFF_KA + h * hd))
        (p,) = _softmax_parts([s])
        o = _dot(p.astype(BF16), cols(OFF_VA + h * hd))
        o_ref[:, OFF_OA + h * hd:OFF_OA + (h + 1) * hd] = o.astype(BF16)

    for hq in range(SW_HEADS):
        kvh = hq // SW_GROUP
        s = _dot_t(cols(OFF_QB + hq * hd), cols(OFF_KB + kvh * hd))
        sink = jnp.full((s.shape[0], 1), sink_ref[hq] * LOG2E, F32)
        (p,) = _softmax_parts([s], extra=sink)
        o = _dot(p.astype(BF16), cols(OFF_VB + kvh * hd))
        o_ref[:, OFF_OB + hq * hd:OFF_OB + (hq + 1) * hd] = o.astype(BF16)

    lam = _df_lambda(lam_ref, lambda_init)
    subln = subln_ref[...]
    for h in range(DF_HEADS):
        w = 2 * hd
        o = _df_core(cols(OFF_QC + h * w, w), [cols(OFF_KC + h * w, w)], [cols(OFF_VC + h * w, w)],
                     lam, subln, lambda_init)
        o_ref[:, OFF_OC + h * w:OFF_OC + (h + 1) * w] = o.astype(BF16)


def _ctx_attn(qkv, att, sink, lam4, subln4, layer, lambda_init, batch, seq, ctx_len):
    ctx_blk0 = batch * seq // ctx_len
    return pl.pallas_call(
        functools.partial(_ctx_body, lambda_init=lambda_init),
        grid=(batch,),
        in_specs=[
            pl.BlockSpec(memory_space=pltpu.SMEM),
            pl.BlockSpec((None, 4, HEAD_DIM), lambda b: (layer, 0, 0)),
            pl.BlockSpec((None, 1, 2 * HEAD_DIM), lambda b: (layer, 0, 0)),
            pl.BlockSpec((ctx_len, IN_WIDTH), lambda b: (ctx_blk0 + b, 0)),
            pl.BlockSpec(memory_space=pl.ANY),
        ],
        out_specs=pl.BlockSpec((ctx_len, ATT_WIDTH), lambda b: (ctx_blk0 + b, 0)),
        out_shape=jax.ShapeDtypeStruct(att.shape, att.dtype),
        input_output_aliases={4: 0},
        compiler_params=_params(("arbitrary",)),
        name="ctx_attn",
    )(sink, lam4, subln4, qkv, att)


def _merge_body(h_ref, att_ref, wg_ref, bg_ref, wb_ref, o_ref, *, branch_width):
    h = h_ref[...]
    acc = None
    for i in range(3):
        gate = jax.nn.sigmoid(_dot(h, wg_ref[i]) + bg_ref[i])
        br = _dot(att_ref[:, i * branch_width:(i + 1) * branch_width], wb_ref[i])
        acc = gate * br if acc is None else acc + gate * br
    o_ref[...] = acc.astype(BF16)


def _merge(h, att, w_gate, b_gate4, w_branch, layer, rows):
    d = h.shape[1]
    nb, bw = w_branch.shape[2], w_branch.shape[3]
    tm, tn = TM_MERGE, TN_MERGE
    return pl.pallas_call(
        functools.partial(_merge_body, branch_width=bw),
        grid=(rows // tm, d // tn),
        in_specs=[
            pl.BlockSpec((tm, d), lambda i, j: (i, 0)),
            pl.BlockSpec((tm, nb * bw), lambda i, j: (i, 0)),
            pl.BlockSpec((None, None, nb, d, tn), lambda i, j: (layer, j, 0, 0, 0)),
            pl.BlockSpec((None, nb, 1, tn), lambda i, j: (layer, 0, 0, j)),
            pl.BlockSpec((None, None, nb, bw, tn), lambda i, j: (layer, j, 0, 0, 0)),
        ],
        out_specs=pl.BlockSpec((tm, tn), lambda i, j: (i, j)),
        out_shape=jax.ShapeDtypeStruct((rows, d), BF16),
        compiler_params=_params(("arbitrary", "arbitrary")),
        name="merge",
    )(h, att, w_gate, b_gate4, w_branch)


@jax.jit
def _forward(x, c, ctx, c_ctx, w_ada, b_ada, norm_g, w_ffn_gate, w_ffn_up, w_ffn_down,
             w_in, na_rpb, sw_sink, df_lambda, df_subln_g, w_branch, w_gate, b_gate, w_out):
    batch, seq, d = x.shape
    ctx_len = ctx.shape[1]
    depth = w_ada.shape[0]
    t_lat = batch * seq
    t = t_lat + batch * ctx_len
    assert seq % TM_UP == 0 and (batch * ctx_len) % TM_UP == 0 and batch + 1 <= MOD_ROWS
    assert seq % ctx_len == 0 and ctx_len % SW_BLOCK == 0

    xs = jnp.concatenate([x.reshape(t_lat, d), ctx.reshape(batch * ctx_len, d)], axis=0)
    cc = jnp.zeros((MOD_ROWS, d), F32).at[:batch].set(c).at[batch].set(c_ctx)
    mod4 = _adaln(cc, w_ada, b_ada).reshape(depth, MOD_ROWS, 1, N_MOD * d)
    gains4 = norm_g.reshape(depth, norm_g.shape[1], 1, d)

    wg16 = _col_tiled(w_ffn_gate, TN_UP)
    wu16 = _col_tiled(w_ffn_up, TN_UP)
    wd16 = w_ffn_down.astype(BF16)
    win16 = _col_tiled(w_in, TN_PROJ)
    wbr16 = _col_tiled(w_branch, TN_MERGE).transpose(0, 2, 1, 3, 4)
    wgt16 = _col_tiled(w_gate, TN_MERGE).transpose(0, 2, 1, 3, 4)
    wout16 = w_out.astype(BF16)
    b_gate4 = b_gate.reshape(depth, b_gate.shape[1], 1, d)
    lam4 = df_lambda.astype(F32)
    subln4 = df_subln_g.reshape(depth, 1, -1)
    cos_t, sin_t = _rope_tables(seq)

    h = _norm_mod(xs, mod4, gains4, 0, seq)
    for l in range(depth):
        last = l == depth - 1
        lambda_init = 0.8 - 0.6 * math.exp(-0.3 * l)
        u = _ffn_up(h, wg16, wu16, l, 0)
        xs, h = _down_epi(u, wd16, (l, 0), xs, t, seq, mod4, gains4, l, 2, 1, 0.5,
                          nxt=(mod4, gains4, l, 2, 3, 4))
        qkv = _proj(h, win16, l, cos_t, sin_t, seq, t_lat)
        rows = t_lat if last else t
        att = jnp.zeros((rows, ATT_WIDTH), BF16)
        att = _na(qkv, att, _na_bias_table(na_rpb[l], seq // GRID_W), batch, seq, ctx_len)
        att = _sw(qkv, att, sw_sink[l], batch, seq, ctx_len)
        att = _df(qkv, att, lam4, subln4, l, lambda_init, batch, seq, ctx_len)
        if not last:
            att = _ctx_attn(qkv, att, sw_sink[l], lam4, subln4, l, lambda_init, batch, seq, ctx_len)
        merged = _merge(h, att, wgt16, b_gate4, wbr16, l, rows)
        xs, h = _down_epi(merged, wout16, (l,), xs, rows, seq, mod4, gains4, l, 5, 3, 1.0,
                          nxt=(mod4, gains4, l, 4, 6, 7))
        u = _ffn_up(h, wg16, wu16, l, 1)
        nxt = None if last else (mod4, gains4, l + 1, 0, 0, 1)
        xs, h = _down_epi(u, wd16, (l, 1), xs, rows, seq, mod4, gains4, l, 8, 5, 0.5, nxt=nxt)
    return xs.reshape(batch, seq, d)


def kernel(x, c, ctx, c_ctx, w_ada, b_ada, norm_g, w_ffn_gate, w_ffn_up, w_ffn_down,
           w_in, na_rpb, sw_sink, df_lambda, df_subln_g, w_branch, w_gate, b_gate, w_out):
    return _forward(x, c, ctx, c_ctx, w_ada, b_ada, norm_g, w_ffn_gate, w_ffn_up, w_ffn_down,
                    w_in, na_rpb, sw_sink, df_lambda, df_subln_g, w_branch, w_gate, b_gate, w_out)
```

```python
import functools
import math

import numpy as np
import jax
import jax.numpy as jnp
from jax import lax
from jax.experimental import pallas as pl
from jax.experimental.pallas import tpu as pltpu

F32 = jnp.float32
BF16 = jnp.bfloat16

GRID_W = 64
HEAD_DIM = 128
NA_HEADS = 8
NA_WIN_ROWS = 8
NA_WIN_COLS = 16
SW_HEADS = 8
SW_KV_HEADS = 2
SW_GROUP = SW_HEADS // SW_KV_HEADS
SW_WINDOW = 128
SW_BLOCK = 128
DF_HEADS = 4
DF_Q_BLOCK = 128
N_MOD = 9
ROPE_BASE = 10000.0
EPS = 1e-6
NEG_INF = -1e30
LOG2E = math.log2(math.e)
Q_SCALE = HEAD_DIM ** -0.5 * LOG2E

OFF_QA, OFF_KA, OFF_VA = 0, 1024, 2048
OFF_QB, OFF_KB, OFF_VB = 3072, 4096, 4352
OFF_QC, OFF_KC, OFF_VC = 4608, 5632, 6656
IN_WIDTH = 7680
OFF_OA, OFF_OB, OFF_OC = 0, 1024, 2048
ATT_WIDTH = 3072

V7X_LANES = 128
MOD_ROWS = 8
V7X_VMEM_LIMIT = 58 * 1024 * 1024

TM_UP, TN_UP = 1024, 512
TM_DOWN, TK_DOWN = 1024, 512
DOWN_READ_STEP = 3
TM_PROJ, TN_PROJ = 1024, 512
DF_Q_STEP = 2 * DF_Q_BLOCK
NA_Q_ROWS = 4
NA_KEY_ROWS = 12
NA_UNROLL = 2
SW_UNROLL = 2
TM_MERGE, TN_MERGE = 1024, 256
TN_ADA = 512
EPI_ROWS = 16
EPI_UNROLL = 8


def _params(sem):
    return pltpu.CompilerParams(dimension_semantics=sem, vmem_limit_bytes=V7X_VMEM_LIMIT)


def _dot(a, b):
    return jnp.dot(a, b, preferred_element_type=F32)


def _dot_t(a, b):
    return lax.dot_general(a, b, (((1,), (1,)), ((), ())), preferred_element_type=F32)


def _rstd(y):
    return lax.rsqrt(jnp.mean(y * y, axis=-1, keepdims=True) + EPS)


def _silu(a):
    return a * jax.nn.sigmoid(a)


def _adaln_body(c_ref, w_ref, b_ref, o_ref):
    s = _silu(c_ref[...]).astype(BF16)
    o_ref[...] = _dot(s, w_ref[...].astype(BF16)) + b_ref[...]


def _adaln(cc, w_ada, b_ada):
    depth, d, nd = w_ada.shape
    return pl.pallas_call(
        _adaln_body,
        grid=(depth, nd // TN_ADA),
        in_specs=[
            pl.BlockSpec((MOD_ROWS, d), lambda l, j: (0, 0)),
            pl.BlockSpec((None, d, TN_ADA), lambda l, j: (l, 0, j)),
            pl.BlockSpec((None, 1, TN_ADA), lambda l, j: (l, 0, j)),
        ],
        out_specs=pl.BlockSpec((None, MOD_ROWS, TN_ADA), lambda l, j: (l, 0, j)),
        out_shape=jax.ShapeDtypeStruct((depth, MOD_ROWS, nd), F32),
        compiler_params=_params(("arbitrary", "arbitrary")),
        name="adaln",
    )(cc, w_ada, b_ada.reshape(depth, 1, nd))


def _mod_spec(layer, chunk, d, tm, seq):
    return pl.BlockSpec((None, None, 1, d), lambda i, *_: (layer, (i * tm) // seq, 0, chunk))


def _gain_spec(layer, idx, d):
    return pl.BlockSpec((None, None, 1, d), lambda i, *_: (layer, idx, 0, 0))


def _norm_mod_body(x_ref, xtail_ref, g_ref, shift_ref, scale_ref, o_ref, *, head_tiles):
    def emit(src_ref):
        x = src_ref[...]
        y = x * _rstd(x) * g_ref[...]
        o_ref[...] = (y * (1.0 + scale_ref[...]) + shift_ref[...]).astype(BF16)

    i = pl.program_id(0)

    @pl.when(i < head_tiles)
    def _():
        emit(x_ref)

    @pl.when(i >= head_tiles)
    def _():
        emit(xtail_ref)


def _norm_mod(x_head, x_tail, mod4, gains4, layer, seq):
    d = x_head.shape[1]
    tm = 256
    head_tiles = x_head.shape[0] // tm
    tail_tiles = x_tail.shape[0] // tm
    return pl.pallas_call(
        functools.partial(_norm_mod_body, head_tiles=head_tiles),
        grid=(head_tiles + tail_tiles,),
        in_specs=[
            pl.BlockSpec((tm, d), lambda i: (jnp.minimum(i, head_tiles - 1), 0)),
            pl.BlockSpec((tm, d), lambda i: (jnp.maximum(i - head_tiles, 0), 0)),
            _gain_spec(layer, 0, d),
            _mod_spec(layer, 0, d, tm, seq),
            _mod_spec(layer, 1, d, tm, seq),
        ],
        out_specs=pl.BlockSpec((tm, d), lambda i: (i, 0)),
        out_shape=jax.ShapeDtypeStruct(((head_tiles + tail_tiles) * tm, d), BF16),
        compiler_params=_params(("arbitrary",)),
        name="norm_mod",
    )(x_head, x_tail, gains4, mod4, mod4)


def _ffn_up_body(h_ref, wg_ref, wu_ref, o_ref):
    h = h_ref[...]
    a = _dot(h, wg_ref[...])
    b = _dot(h, wu_ref[...])
    o_ref[...] = (_silu(a) * b).astype(BF16)


def _ffn_up(h, w_gate, w_up, layer, which):
    t, d = h.shape
    n_tiles = w_gate.shape[-1] // TN_UP
    wspec = pl.BlockSpec((None, None, d, TN_UP), lambda i, j: (layer, which, 0, j))
    return pl.pallas_call(
        _ffn_up_body,
        grid=(t // TM_UP, n_tiles),
        in_specs=[pl.BlockSpec((TM_UP, d), lambda i, j: (i, 0)), wspec, wspec],
        out_specs=pl.BlockSpec((None, TM_UP, TN_UP), lambda i, j: (j, i, 0)),
        out_shape=jax.ShapeDtypeStruct((n_tiles, t, TN_UP), BF16),
        compiler_params=_params(("arbitrary", "arbitrary")),
        name="ffn_up",
    )(h, w_gate, w_up)


def _down_epi_body(a_ref, w_ref, x_hbm, xtail_hbm, gate_ref, gpost_ref, *rest, coef, nk, tm, with_next,
                   head_tiles):
    if with_next:
        (gnext_ref, shift_ref, scale_ref, xo_hbm, ho_hbm,
         acc_ref, xbuf_ref, stat_ref, sem_ref, hbuf_ref) = rest
    else:
        xo_hbm, acc_ref, xbuf_ref, stat_ref, sem_ref = rest
    i = pl.program_id(0)
    k = pl.program_id(1)
    n_tiles = pl.num_programs(0)
    d = acc_ref.shape[-1]
    sem_in, sem_xout, sem_hout = 0, 1, 2

    def tile_rows(tile):
        return pl.ds(pl.multiple_of(tile * tm, tm), tm)

    def residual_copy(src, tile):
        return pltpu.make_async_copy(src.at[tile_rows(tile), :], xbuf_ref, sem_ref.at[sem_in])

    def residual_read(tile, action):
        if head_tiles is None:
            action(residual_copy(x_hbm, tile))
        else:
            @pl.when(tile < head_tiles)
            def _():
                action(residual_copy(x_hbm, tile))

            @pl.when(tile >= head_tiles)
            def _():
                action(residual_copy(xtail_hbm, tile - head_tiles))

    def x_writeback(tile):
        return pltpu.make_async_copy(xbuf_ref, xo_hbm.at[tile_rows(tile), :], sem_ref.at[sem_xout])

    def h_writeback(tile):
        return pltpu.make_async_copy(hbuf_ref, ho_hbm.at[tile_rows(tile), :], sem_ref.at[sem_hout])

    @pl.when(k == 0)
    def _():
        acc_ref[...] = _dot(a_ref[...], w_ref[...])

    @pl.when(jnp.logical_and(k != 0, k != nk - 1))
    def _():
        acc_ref[...] += _dot(a_ref[...], w_ref[...])

    @pl.when(k == nk - 1)
    def _():
        y = acc_ref[...] + _dot(a_ref[...], w_ref[...])
        acc_ref[...] = y
        stat_ref[0] = jnp.broadcast_to(_rstd(y), (tm, stat_ref.shape[-1]))

    @pl.when(k == DOWN_READ_STEP)
    def _():
        @pl.when(i > 0)
        def _():
            x_writeback(i - 1).wait()

        residual_read(i, lambda copy: copy.start())

    @pl.when(k == nk - 1)
    def _():
        gate_gain = (coef * gate_ref[...]) * gpost_ref[...]
        if with_next:
            next_gain = gnext_ref[...] * (1.0 + scale_ref[...])
            shift = shift_ref[...]

        lanes = stat_ref.shape[-1]

        def chunk_rows(c):
            return pl.ds(pl.multiple_of(c * EPI_ROWS, EPI_ROWS), EPI_ROWS)

        def put_stat(slot, rows, val):
            stat_ref[slot, rows, :] = jnp.broadcast_to(val, (EPI_ROWS, lanes))

        def get_stat(slot, rows):
            return jnp.tile(stat_ref[slot, rows, :], (1, d // lanes))

        def pass_residual(c, carry):
            rows = chunk_rows(c)
            xn = xbuf_ref[rows, :] + (acc_ref[rows, :] * get_stat(0, rows)) * gate_gain
            xbuf_ref[rows, :] = xn
            if with_next:
                put_stat(1, rows, _rstd(xn))
            return carry

        def pass_modulate(c, carry):
            rows = chunk_rows(c)
            hn = (xbuf_ref[rows, :] * get_stat(1, rows)) * next_gain + shift
            hbuf_ref[rows, :] = hn.astype(BF16)
            return carry

        n_chunks = tm // EPI_ROWS
        residual_read(i, lambda copy: copy.wait())
        lax.fori_loop(0, n_chunks, pass_residual, 0, unroll=EPI_UNROLL)
        x_writeback(i).start()
        if with_next:
            @pl.when(i > 0)
            def _():
                h_writeback(i - 1).wait()

            lax.fori_loop(0, n_chunks, pass_modulate, 0, unroll=EPI_UNROLL)
            h_writeback(i).start()

        @pl.when(i == n_tiles - 1)
        def _():
            x_writeback(i).wait()
            if with_next:
                h_writeback(i).wait()


def _down_epi(a, w, wsel, xs, rows, seq, mod4, gains4, layer, gate_chunk, gpost_idx, coef,
              nxt=None):
    tm, tk = TM_DOWN, TK_DOWN
    if isinstance(xs, tuple):
        x_head, x_tail = xs
        assert x_head.shape[0] % tm == 0 and x_tail.shape[0] % tm == 0
        head_tiles = x_head.shape[0] // tm
    else:
        x_head, x_tail, head_tiles = xs, xs, None
    d = x_head.shape[1]
    if a.ndim == 3:
        assert a.shape[2] == tk
        nk = a.shape[0]
        aspec = pl.BlockSpec((None, tm, tk), lambda i, k: (k, i, 0))
    else:
        nk = a.shape[1] // tk
        aspec = pl.BlockSpec((tm, tk), lambda i, k: (i, k))
    assert nk - 1 > DOWN_READ_STEP and rows % tm == 0
    nlead = len(wsel)
    wspec = pl.BlockSpec((None,) * nlead + (tk, d), lambda i, k: tuple(wsel) + (k, 0))
    hbm = pl.BlockSpec(memory_space=pl.ANY)
    in_specs = [
        aspec,
        wspec,
        hbm,
        hbm,
        _mod_spec(layer, gate_chunk, d, tm, seq),
        _gain_spec(layer, gpost_idx, d),
    ]
    args = [a, w, x_head, x_tail, mod4, gains4]
    out_specs = [hbm]
    out_shape = [jax.ShapeDtypeStruct((rows, d), F32)]
    scratch = [
        pltpu.VMEM((tm, d), F32),
        pltpu.VMEM((tm, d), F32),
        pltpu.VMEM((2, tm, V7X_LANES), F32),
        pltpu.SemaphoreType.DMA((3,)),
    ]
    if nxt is not None:
        mod_n, gains_n, layer_n, gain_idx, shift_chunk, scale_chunk = nxt
        in_specs += [
            _gain_spec(layer_n, gain_idx, d),
            _mod_spec(layer_n, shift_chunk, d, tm, seq),
            _mod_spec(layer_n, scale_chunk, d, tm, seq),
        ]
        args += [gains_n, mod_n, mod_n]
        out_specs.append(hbm)
        out_shape.append(jax.ShapeDtypeStruct((rows, d), BF16))
        scratch.append(pltpu.VMEM((tm, d), BF16))
    body = functools.partial(_down_epi_body, coef=coef, nk=nk, tm=tm, with_next=nxt is not None,
                             head_tiles=head_tiles)
    out = pl.pallas_call(
        body,
        grid=(rows // tm, nk),
        in_specs=in_specs,
        out_specs=out_specs,
        out_shape=out_shape,
        scratch_shapes=scratch,
        compiler_params=_params(("arbitrary", "arbitrary")),
        name="down_epi",
    )(*args)
    return (out[0], out[1]) if nxt is not None else (out[0], None)


def _in_tiles(j, lo, hi):
    return jnp.logical_and(j >= lo // TN_PROJ, j < hi // TN_PROJ)


def _proj_body(h_ref, w_ref, cos_ref, sin_ref, o_ref, *, n_lat_tiles):
    i = pl.program_id(0)
    j = pl.program_id(1)
    hd = HEAD_DIM
    acc = _dot(h_ref[...], w_ref[...])
    is_q = _in_tiles(j, OFF_QA, OFF_KA) | _in_tiles(j, OFF_QB, OFF_KB) | _in_tiles(j, OFF_QC, OFF_KC)
    qscale = jnp.where(is_q, Q_SCALE, 1.0).astype(F32)
    latent = i < n_lat_tiles
    rope_full = latent & (_in_tiles(j, OFF_QB, OFF_KB) | _in_tiles(j, OFF_QC, OFF_VC))
    rope_half = latent & _in_tiles(j, OFF_KB, OFF_QC)

    def rope_heads(n_heads, scale=None):
        cos = cos_ref[...]
        sin = sin_ref[...]
        if scale is not None:
            cos = cos * scale
            sin = sin * scale
        lane = lax.broadcasted_iota(jnp.int32, cos.shape, 1)
        quarter = hd // 4
        first = (lane % (2 * quarter)) < quarter
        for hh in range(n_heads):
            a = acc[:, hh * hd:(hh + 1) * hd]
            partner = jnp.where(first, pltpu.roll(a, hd - quarter, 1), pltpu.roll(a, quarter, 1))
            o_ref[:, hh * hd:(hh + 1) * hd] = (a * cos + partner * sin).astype(BF16)

    @pl.when(rope_full)
    def _():
        rope_heads(TN_PROJ // hd, qscale)

    @pl.when(rope_half)
    def _():
        n_rope = (OFF_VB - OFF_KB) // hd
        rope_heads(n_rope)
        o_ref[:, n_rope * hd:] = acc[:, n_rope * hd:].astype(BF16)

    @pl.when(jnp.logical_not(rope_full | rope_half))
    def _():
        o_ref[...] = (acc * qscale).astype(BF16)


def _proj(h, w_in, layer, cos_t, sin_t, seq, t_lat):
    t, d = h.shape
    width = w_in.shape[-1]
    n_tiles = width // TN_PROJ
    pos_tiles = seq // TM_PROJ
    assert OFF_KB % TN_PROJ == 0 and OFF_QC - OFF_KB == TN_PROJ
    tspec = pl.BlockSpec((TM_PROJ, HEAD_DIM), lambda i, j: (i % pos_tiles, 0))
    return pl.pallas_call(
        functools.partial(_proj_body, n_lat_tiles=t_lat // TM_PROJ),
        grid=(t // TM_PROJ, n_tiles),
        in_specs=[
            pl.BlockSpec((TM_PROJ, d), lambda i, j: (i, 0)),
            pl.BlockSpec((None, d, TN_PROJ), lambda i, j: (layer, 0, j)),
            tspec, tspec,
        ],
        out_specs=pl.BlockSpec((TM_PROJ, TN_PROJ), lambda i, j: (i, j)),
        out_shape=jax.ShapeDtypeStruct((t, width), BF16),
        compiler_params=_params(("arbitrary", "arbitrary")),
        name="proj_rope",
    )(h, w_in, cos_t, sin_t)


def _rope_tables(n):
    t = jnp.arange(n, dtype=jnp.int32)
    row = (t // GRID_W).astype(F32)
    col = (t % GRID_W).astype(F32)
    half = HEAD_DIM // 2
    inv = ROPE_BASE ** (-jnp.arange(0, half, 2, dtype=F32) / half)
    ang_r = row[:, None] * inv
    ang_c = col[:, None] * inv
    cos_h = jnp.concatenate([jnp.cos(ang_r)] * 2 + [jnp.cos(ang_c)] * 2, axis=-1)
    sin_h = jnp.concatenate([-jnp.sin(ang_r), jnp.sin(ang_r), -jnp.sin(ang_c), jnp.sin(ang_c)], axis=-1)
    return cos_h, sin_h


def _na_key_row_start(blk, rows):
    return np.clip(blk * NA_Q_ROWS - NA_WIN_ROWS // 2, 0, rows - NA_KEY_ROWS)


def _na_bias_table(rpb, rows):
    kh, kw = NA_WIN_ROWS, NA_WIN_COLS
    nblk = rows // NA_Q_ROWS
    assert rows % NA_Q_ROWS == 0 and nblk >= 3 and rows >= NA_KEY_ROWS
    qc = np.arange(GRID_W)
    kc = np.arange(GRID_W)
    col_start = np.clip(qc - kw // 2, 0, GRID_W - kw)
    col_ok = (kc[None, :] >= col_start[:, None]) & (kc[None, :] < col_start[:, None] + kw)
    dc = kc[None, :] - qc[:, None] + NA_WIN_COLS - 1
    col_hot = (dc[None] == np.arange(2 * NA_WIN_COLS - 1)[:, None, None]).astype(np.float32)
    n_dr = 2 * NA_WIN_ROWS - 1
    row_hot = np.zeros((3, NA_Q_ROWS, NA_KEY_ROWS, n_dr), np.float32)
    for cls, blk in enumerate((0, 1, nblk - 1)):
        ks = _na_key_row_start(blk, rows)
        for j in range(NA_Q_ROWS):
            r = blk * NA_Q_ROWS + j
            rs = np.clip(r - kh // 2, 0, rows - kh)
            for i in range(NA_KEY_ROWS):
                if rs <= ks + i < rs + kh:
                    row_hot[cls, j, i, ks + i - r + NA_WIN_ROWS - 1] = 1.0
    b = jnp.einsum('cjir,hrd,dqk->hcjqik', row_hot, rpb.astype(F32) * LOG2E, col_hot,
                   precision=lax.Precision.HIGHEST)
    ok = (row_hot.sum(-1) > 0)[:, :, None, :, None] & col_ok[None, None, :, None, :]
    b = jnp.where(ok[None], b, NEG_INF)
    return b.reshape(rpb.shape[0], 3, NA_Q_ROWS * GRID_W, NA_KEY_ROWS * GRID_W)


def _softmax_parts(parts, extra=None):
    m = functools.reduce(jnp.maximum, [jnp.max(p, axis=-1, keepdims=True) for p in parts])
    if extra is not None:
        m = jnp.maximum(m, extra)
    es = [jnp.exp2(p - m) for p in parts]
    l = functools.reduce(jnp.add, [jnp.sum(e, axis=-1, keepdims=True) for e in es])
    if extra is not None:
        l = l + jnp.exp2(extra - m)
    inv = 1.0 / l
    return [e * inv for e in es]


def _na_body(q_ref, k_ref, v_ref, kx_ref, vx_ref, bias_ref, att_in_ref, o_ref, *, rows):
    del att_in_ref
    nblk = rows // NA_Q_ROWS
    nq = NA_Q_ROWS * GRID_W
    nkeys = NA_KEY_ROWS * GRID_W
    kx = kx_ref[...]
    vx = vx_ref[...]

    def blk_fn(bi, carry):
        ks = jnp.clip(bi * NA_Q_ROWS - NA_WIN_ROWS // 2, 0, rows - NA_KEY_ROWS)
        cls = jnp.where(bi == 0, 0, jnp.where(bi == nblk - 1, 2, 1))
        qrows = pl.ds(pl.multiple_of(bi * nq, nq), nq)
        krows = pl.ds(pl.multiple_of(ks * GRID_W, GRID_W), nkeys)
        q = q_ref[qrows, :]
        s_loc = _dot_t(q, k_ref[krows, :]) + bias_ref[cls]
        s_ctx = _dot_t(q, kx)
        p_loc, p_ctx = _softmax_parts([s_loc, s_ctx])
        o = _dot(p_loc.astype(BF16), v_ref[krows, :]) + _dot(p_ctx.astype(BF16), vx)
        o_ref[qrows, :] = o.astype(BF16)
        return carry

    lax.fori_loop(0, nblk, blk_fn, 0, unroll=NA_UNROLL)


def _na(qkv, att, bias_tbl, batch, seq, ctx_len):
    rows = seq // GRID_W
    hd = HEAD_DIM
    ctx_blk0 = batch * seq // ctx_len

    def col(off):
        return off // hd

    return pl.pallas_call(
        functools.partial(_na_body, rows=rows),
        grid=(batch, NA_HEADS),
        in_specs=[
            pl.BlockSpec((seq, hd), lambda b, h: (b, col(OFF_QA) + h)),
            pl.BlockSpec((seq, hd), lambda b, h: (b, col(OFF_KA) + h)),
            pl.BlockSpec((seq, hd), lambda b, h: (b, col(OFF_VA) + h)),
            pl.BlockSpec((ctx_len, hd), lambda b, h: (ctx_blk0 + b, col(OFF_KA) + h)),
            pl.BlockSpec((ctx_len, hd), lambda b, h: (ctx_blk0 + b, col(OFF_VA) + h)),
            pl.BlockSpec((None,) + bias_tbl.shape[1:], lambda b, h: (h, 0, 0, 0)),
            pl.BlockSpec(memory_space=pl.ANY),
        ],
        out_specs=pl.BlockSpec((seq, hd), lambda b, h: (b, col(OFF_OA) + h)),
        out_shape=jax.ShapeDtypeStruct(att.shape, att.dtype),
        input_output_aliases={6: 0},
        compiler_params=_params(("arbitrary", "arbitrary")),
        name="na_attn",
    )(qkv, qkv, qkv, qkv, qkv, bias_tbl, att)


def _sw_body(sink_ref, q_ref, k_ref, v_ref, kx_ref, vx_ref, att_in_ref, o_ref, *, seq):
    del att_in_ref
    kv = pl.program_id(1)
    blk = SW_BLOCK
    span = 3 * blk
    g = SW_GROUP
    kx = kx_ref[...]
    vx = vx_ref[...]
    grp = lax.broadcasted_iota(jnp.int32, (g * blk, 1), 0) // blk
    sink_col = jnp.zeros((g * blk, 1), F32)
    for gi in range(g):
        sink_col = jnp.where(grp == gi, sink_ref[kv * g + gi] * LOG2E, sink_col)
    qoff = lax.broadcasted_iota(jnp.int32, (g * blk, span), 0) % blk
    koff = lax.broadcasted_iota(jnp.int32, (g * blk, span), 1)

    def blk_fn(n, carry):
        start = jnp.clip((n - 1) * blk, 0, seq - span)
        qrows = pl.ds(pl.multiple_of(n * blk, blk), blk)
        krows = pl.ds(pl.multiple_of(start, blk), span)
        qb = q_ref[qrows, :]
        qs = jnp.concatenate([qb[:, gi * HEAD_DIM:(gi + 1) * HEAD_DIM] for gi in range(g)], axis=0)
        s_loc = _dot_t(qs, k_ref[krows, :])
        dist = (n * blk + qoff) - (start + koff)
        s_loc = jnp.where(jnp.abs(dist) <= SW_WINDOW, s_loc, NEG_INF)
        s_ctx = _dot_t(qs, kx)
        p_loc, p_ctx = _softmax_parts([s_loc, s_ctx], extra=sink_col)
        o = _dot(p_loc.astype(BF16), v_ref[krows, :]) + _dot(p_ctx.astype(BF16), vx)
        for gi in range(g):
            o_ref[qrows, gi * HEAD_DIM:(gi + 1) * HEAD_DIM] = o[gi * blk:(gi + 1) * blk].astype(BF16)
        return carry

    lax.fori_loop(0, seq // blk, blk_fn, 0, unroll=SW_UNROLL)


def _sw(qkv, att, sink, batch, seq, ctx_len):
    hd = HEAD_DIM
    gw = SW_GROUP * hd
    ctx_blk0 = batch * seq // ctx_len
    return pl.pallas_call(
        functools.partial(_sw_body, seq=seq),
        grid=(batch, SW_KV_HEADS),
        in_specs=[
            pl.BlockSpec(memory_space=pltpu.SMEM),
            pl.BlockSpec((seq, gw), lambda b, k: (b, OFF_QB // gw + k)),
            pl.BlockSpec((seq, hd), lambda b, k: (b, OFF_KB // hd + k)),
            pl.BlockSpec((seq, hd), lambda b, k: (b, OFF_VB // hd + k)),
            pl.BlockSpec((ctx_len, hd), lambda b, k: (ctx_blk0 + b, OFF_KB // hd + k)),
            pl.BlockSpec((ctx_len, hd), lambda b, k: (ctx_blk0 + b, OFF_VB // hd + k)),
            pl.BlockSpec(memory_space=pl.ANY),
        ],
        out_specs=pl.BlockSpec((seq, gw), lambda b, k: (b, OFF_OB // gw + k)),
        out_shape=jax.ShapeDtypeStruct(att.shape, att.dtype),
        input_output_aliases={6: 0},
        compiler_params=_params(("arbitrary", "arbitrary")),
        name="sw_attn",
    )(sink, qkv, qkv, qkv, qkv, qkv, att)


def _df_lambda(lam_ref, lambda_init):
    lv = lam_ref[...]
    s01 = jnp.sum(lv[0:1] * lv[1:2], axis=-1, keepdims=True)
    s23 = jnp.sum(lv[2:3] * lv[3:4], axis=-1, keepdims=True)
    return jnp.exp(s01) - jnp.exp(s23) + lambda_init


def _df_core(q, k_parts, v_parts, lam, subln, lambda_init):
    hd = HEAD_DIM
    exps, weights = [], []
    for c in range(2):
        qc = q[:, c * hd:(c + 1) * hd]
        scores = [_dot_t(qc, kp[:, c * hd:(c + 1) * hd]) for kp in k_parts]
        m = functools.reduce(jnp.maximum, [jnp.max(s, axis=-1, keepdims=True) for s in scores])
        es = [jnp.exp2(s - m) for s in scores]
        l = functools.reduce(jnp.add, [jnp.sum(e, axis=-1, keepdims=True) for e in es])
        exps.append(es)
        weights.append(1.0 / l)
    w0 = weights[0]
    w1 = lam * weights[1]
    o = None
    for i, vp in enumerate(v_parts):
        a = (exps[0][i] * w0 - exps[1][i] * w1).astype(BF16)
        term = _dot(a, vp[...])
        o = term if o is None else o + term
    return o * _rstd(o) * subln * (1.0 - lambda_init)


def _df_body(lam_ref, subln_ref, q_ref, k_ref, v_ref, kx_ref, vx_ref, att_in_ref, o_ref, *,
             lambda_init):
    del att_in_ref
    lam = _df_lambda(lam_ref, lambda_init)
    subln = subln_ref[...]
    for sub in range(DF_Q_STEP // DF_Q_BLOCK):
        rows = slice(sub * DF_Q_BLOCK, (sub + 1) * DF_Q_BLOCK)
        o = _df_core(q_ref[rows, :], [k_ref, kx_ref], [v_ref, vx_ref], lam, subln, lambda_init)
        o_ref[rows, :] = o.astype(BF16)


def _df(qkv, att, lam4, subln4, layer, lambda_init, batch, seq, ctx_len):
    w = 2 * HEAD_DIM
    nqb = seq // DF_Q_STEP
    ctx_blk0 = batch * seq // ctx_len
    return pl.pallas_call(
        functools.partial(_df_body, lambda_init=lambda_init),
        grid=(batch, DF_HEADS, nqb),
        in_specs=[
            pl.BlockSpec((None, 4, HEAD_DIM), lambda b, h, n: (layer, 0, 0)),
            pl.BlockSpec((None, 1, w), lambda b, h, n: (layer, 0, 0)),
            pl.BlockSpec((DF_Q_STEP, w), lambda b, h, n: (b * nqb + n, OFF_QC // w + h)),
            pl.BlockSpec((seq, w), lambda b, h, n: (b, OFF_KC // w + h)),
            pl.BlockSpec((seq, w), lambda b, h, n: (b, OFF_VC // w + h)),
            pl.BlockSpec((ctx_len, w), lambda b, h, n: (ctx_blk0 + b, OFF_KC // w + h)),
            pl.BlockSpec((ctx_len, w), lambda b, h, n: (ctx_blk0 + b, OFF_VC // w + h)),
            pl.BlockSpec(memory_space=pl.ANY),
        ],
        out_specs=pl.BlockSpec((DF_Q_STEP, w), lambda b, h, n: (b * nqb + n, OFF_OC // w + h)),
        out_shape=jax.ShapeDtypeStruct(att.shape, att.dtype),
        input_output_aliases={7: 0},
        compiler_params=_params(("arbitrary", "arbitrary", "arbitrary")),
        name="df_attn",
    )(lam4, subln4, qkv, qkv, qkv, qkv, qkv, att)


def _ctx_body(sink_ref, lam_ref, subln_ref, x_ref, att_in_ref, o_ref, *, lambda_init):
    del att_in_ref
    hd = HEAD_DIM

    def cols(off, width=hd):
        return x_ref[:, off:off + width]

    for h in range(NA_HEADS):
        s = _dot_t(cols(OFF_QA + h * hd), cols(OFF_KA + h * hd))
        (p,) = _softmax_parts([s])
        o = _dot(p.astype(BF16), cols(OFF_VA + h * hd))
        o_ref[:, OFF_OA + h * hd:OFF_OA + (h + 1) * hd] = o.astype(BF16)

    for hq in range(SW_HEADS):
        kvh = hq // SW_GROUP
        s = _dot_t(cols(OFF_QB + hq * hd), cols(OFF_KB + kvh * hd))
        sink = jnp.full((s.shape[0], 1), sink_ref[hq] * LOG2E, F32)
        (p,) = _softmax_parts([s], extra=sink)
        o = _dot(p.astype(BF16), cols(OFF_VB + kvh * hd))
        o_ref[:, OFF_OB + hq * hd:OFF_OB + (hq + 1) * hd] = o.astype(BF16)

    lam = _df_lambda(lam_ref, lambda_init)
    subln = subln_ref[...]
    for h in range(DF_HEADS):
        w = 2 * hd
        o = _df_core(cols(OFF_QC + h * w, w), [cols(OFF_KC + h * w, w)], [cols(OFF_VC + h * w, w)],
                     lam, subln, lambda_init)
        o_ref[:, OFF_OC + h * w:OFF_OC + (h + 1) * w] = o.astype(BF16)


def _ctx_attn(qkv, att, sink, lam4, subln4, layer, lambda_init, batch, seq, ctx_len):
    ctx_blk0 = batch * seq // ctx_len
    return pl.pallas_call(
        functools.partial(_ctx_body, lambda_init=lambda_init),
        grid=(batch,),
        in_specs=[
            pl.BlockSpec(memory_space=pltpu.SMEM),
            pl.BlockSpec((None, 4, HEAD_DIM), lambda b: (layer, 0, 0)),
            pl.BlockSpec((None, 1, 2 * HEAD_DIM), lambda b: (layer, 0, 0)),
            pl.BlockSpec((ctx_len, IN_WIDTH), lambda b: (ctx_blk0 + b, 0)),
            pl.BlockSpec(memory_space=pl.ANY),
        ],
        out_specs=pl.BlockSpec((ctx_len, ATT_WIDTH), lambda b: (ctx_blk0 + b, 0)),
        out_shape=jax.ShapeDtypeStruct(att.shape, att.dtype),
        input_output_aliases={4: 0},
        compiler_params=_params(("arbitrary",)),
        name="ctx_attn",
    )(sink, lam4, subln4, qkv, att)


def _merge_body(h_ref, att_ref, wg_ref, bg_ref, wb_ref, o_ref, *, branch_width):
    h = h_ref[...]
    acc = None
    for i in range(3):
        gate = jax.nn.sigmoid(_dot(h, wg_ref[i]) + bg_ref[i])
        br = _dot(att_ref[:, i * branch_width:(i + 1) * branch_width], wb_ref[i])
        acc = gate * br if acc is None else acc + gate * br
    o_ref[...] = acc.astype(BF16)


def _merge(h, att, w_gate, b_gate4, w_branch, layer, rows):
    d = h.shape[1]
    nb, bw = w_branch.shape[1], w_branch.shape[2]
    tm, tn = TM_MERGE, TN_MERGE
    return pl.pallas_call(
        functools.partial(_merge_body, branch_width=bw),
        grid=(rows // tm, d // tn),
        in_specs=[
            pl.BlockSpec((tm, d), lambda i, j: (i, 0)),
            pl.BlockSpec((tm, nb * bw), lambda i, j: (i, 0)),
            pl.BlockSpec((None, nb, d, tn), lambda i, j: (layer, 0, 0, j)),
            pl.BlockSpec((None, nb, 1, tn), lambda i, j: (layer, 0, 0, j)),
            pl.BlockSpec((None, nb, bw, tn), lambda i, j: (layer, 0, 0, j)),
        ],
        out_specs=pl.BlockSpec((tm, tn), lambda i, j: (i, j)),
        out_shape=jax.ShapeDtypeStruct((rows, d), BF16),
        compiler_params=_params(("arbitrary", "arbitrary")),
        name="merge",
    )(h, att, w_gate, b_gate4, w_branch)


@jax.jit
def _forward(x, c, ctx, c_ctx, w_ada, b_ada, norm_g, w_ffn_gate, w_ffn_up, w_ffn_down,
             w_in, na_rpb, sw_sink, df_lambda, df_subln_g, w_branch, w_gate, b_gate, w_out):
    batch, seq, d = x.shape
    ctx_len = ctx.shape[1]
    depth = w_ada.shape[0]
    t_lat = batch * seq
    t = t_lat + batch * ctx_len
    assert seq % TM_UP == 0 and (batch * ctx_len) % TM_UP == 0 and batch + 1 <= MOD_ROWS
    assert seq % ctx_len == 0 and ctx_len % SW_BLOCK == 0

    xs = (x.reshape(t_lat, d), ctx.reshape(batch * ctx_len, d))
    cc = jnp.zeros((MOD_ROWS, d), F32).at[:batch].set(c).at[batch].set(c_ctx)
    mod4 = _adaln(cc, w_ada, b_ada).reshape(depth, MOD_ROWS, 1, N_MOD * d)
    gains4 = norm_g.reshape(depth, norm_g.shape[1], 1, d)

    wg16 = w_ffn_gate.astype(BF16)
    wu16 = w_ffn_up.astype(BF16)
    wd16 = w_ffn_down.astype(BF16)
    win16 = w_in.astype(BF16)
    wbr16 = w_branch.astype(BF16)
    wgt16 = w_gate.astype(BF16)
    wout16 = w_out.astype(BF16)
    b_gate4 = b_gate.reshape(depth, b_gate.shape[1], 1, d)
    lam4 = df_lambda.astype(F32)
    subln4 = df_subln_g.reshape(depth, 1, -1)
    cos_t, sin_t = _rope_tables(seq)

    h = _norm_mod(xs[0], xs[1], mod4, gains4, 0, seq)
    for l in range(depth):
        last = l == depth - 1
        lambda_init = 0.8 - 0.6 * math.exp(-0.3 * l)
        u = _ffn_up(h, wg16, wu16, l, 0)
        xs, h = _down_epi(u, wd16, (l, 0), xs, t, seq, mod4, gains4, l, 2, 1, 0.5,
                          nxt=(mod4, gains4, l, 2, 3, 4))
        qkv = _proj(h, win16, l, cos_t, sin_t, seq, t_lat)
        rows = t_lat if last else t
        att = jnp.zeros((rows, ATT_WIDTH), BF16)
        att = _na(qkv, att, _na_bias_table(na_rpb[l], seq // GRID_W), batch, seq, ctx_len)
        att = _sw(qkv, att, sw_sink[l], batch, seq, ctx_len)
        att = _df(qkv, att, lam4, subln4, l, lambda_init, batch, seq, ctx_len)
        if not last:
            att = _ctx_attn(qkv, att, sw_sink[l], lam4, subln4, l, lambda_init, batch, seq, ctx_len)
        merged = _merge(h, att, wgt16, b_gate4, wbr16, l, rows)
        xs, h = _down_epi(merged, wout16, (l,), xs, rows, seq, mod4, gains4, l, 5, 3, 1.0,
                          nxt=(mod4, gains4, l, 4, 6, 7))
        u = _ffn_up(h, wg16, wu16, l, 1)
        nxt = None if last else (mod4, gains4, l + 1, 0, 0, 1)
        xs, h = _down_epi(u, wd16, (l, 1), xs, rows, seq, mod4, gains4, l, 8, 5, 0.5, nxt=nxt)
    return xs.reshape(batch, seq, d)


def kernel(x, c, ctx, c_ctx, w_ada, b_ada, norm_g, w_ffn_gate, w_ffn_up, w_ffn_down,
           w_in, na_rpb, sw_sink, df_lambda, df_subln_g, w_branch, w_gate, b_gate, w_out):
    return _forward(x, c, ctx, c_ctx, w_ada, b_ada, norm_g, w_ffn_gate, w_ffn_up, w_ffn_down,
                    w_in, na_rpb, sw_sink, df_lambda, df_subln_g, w_branch, w_gate, b_gate, w_out)
```

```python
import functools
import math

import numpy as np
import jax
import jax.numpy as jnp
from jax import lax
from jax.experimental import pallas as pl
from jax.experimental.pallas import tpu as pltpu

F32 = jnp.float32
BF16 = jnp.bfloat16

GRID_W = 64
HEAD_DIM = 128
NA_HEADS = 8
NA_WIN_ROWS = 8
NA_WIN_COLS = 16
SW_HEADS = 8
SW_KV_HEADS = 2
SW_GROUP = SW_HEADS // SW_KV_HEADS
SW_WINDOW = 128
SW_BLOCK = 128
DF_HEADS = 4
DF_Q_BLOCK = 128
N_MOD = 9
ROPE_BASE = 10000.0
EPS = 1e-6
NEG_INF = -1e30
LOG2E = math.log2(math.e)
Q_SCALE = HEAD_DIM ** -0.5 * LOG2E

OFF_QA, OFF_KA, OFF_VA = 0, 1024, 2048
OFF_QB, OFF_KB, OFF_VB = 3072, 4096, 4352
OFF_QC, OFF_KC, OFF_VC = 4608, 5632, 6656
IN_WIDTH = 7680
OFF_OA, OFF_OB, OFF_OC = 0, 1024, 2048
ATT_WIDTH = 3072

V7X_LANES = 128
MOD_ROWS = 8
V7X_VMEM_LIMIT = 58 * 1024 * 1024

TM_UP, TN_UP = 1024, 512
TM_DOWN, TK_DOWN = 1024, 512
DOWN_READ_STEP = 3
TM_PROJ, TN_PROJ = 1024, 512
DF_Q_STEP = 8 * DF_Q_BLOCK
NA_Q_ROWS = 4
NA_KEY_ROWS = 12
NA_UNROLL = 4
SW_UNROLL = 4
TM_MERGE, TN_MERGE = 1024, 256
TN_ADA = 512
EPI_ROWS = 16
EPI_UNROLL = 8


def _params(sem):
    return pltpu.CompilerParams(dimension_semantics=sem, vmem_limit_bytes=V7X_VMEM_LIMIT)


def _dot(a, b):
    return jnp.dot(a, b, preferred_element_type=F32)


def _dot_t(a, b):
    return lax.dot_general(a, b, (((1,), (1,)), ((), ())), preferred_element_type=F32)


def _rstd(y):
    return lax.rsqrt(jnp.mean(y * y, axis=-1, keepdims=True) + EPS)


def _silu(a):
    return a * jax.nn.sigmoid(a)


def _adaln_body(c_ref, w_ref, b_ref, o_ref):
    s = _silu(c_ref[...]).astype(BF16)
    o_ref[...] = _dot(s, w_ref[...].astype(BF16)) + b_ref[...]


def _adaln(cc, w_ada, b_ada):
    depth, d, nd = w_ada.shape
    return pl.pallas_call(
        _adaln_body,
        grid=(depth, nd // TN_ADA),
        in_specs=[
            pl.BlockSpec((MOD_ROWS, d), lambda l, j: (0, 0)),
            pl.BlockSpec((None, d, TN_ADA), lambda l, j: (l, 0, j)),
            pl.BlockSpec((None, 1, TN_ADA), lambda l, j: (l, 0, j)),
        ],
        out_specs=pl.BlockSpec((None, MOD_ROWS, TN_ADA), lambda l, j: (l, 0, j)),
        out_shape=jax.ShapeDtypeStruct((depth, MOD_ROWS, nd), F32),
        compiler_params=_params(("arbitrary", "arbitrary")),
        name="adaln",
    )(cc, w_ada, b_ada.reshape(depth, 1, nd))


def _mod_spec(layer, chunk, d, tm, seq):
    return pl.BlockSpec((None, None, 1, d), lambda i, *_: (layer, (i * tm) // seq, 0, chunk))


def _gain_spec(layer, idx, d):
    return pl.BlockSpec((None, None, 1, d), lambda i, *_: (layer, idx, 0, 0))


def _norm_mod_body(x_ref, xtail_ref, g_ref, shift_ref, scale_ref, o_ref, *, head_tiles):
    def emit(src_ref):
        x = src_ref[...]
        y = x * _rstd(x) * g_ref[...]
        o_ref[...] = (y * (1.0 + scale_ref[...]) + shift_ref[...]).astype(BF16)

    i = pl.program_id(0)

    @pl.when(i < head_tiles)
    def _():
        emit(x_ref)

    @pl.when(i >= head_tiles)
    def _():
        emit(xtail_ref)


def _norm_mod(x_head, x_tail, mod4, gains4, layer, seq):
    d = x_head.shape[1]
    tm = 256
    head_tiles = x_head.shape[0] // tm
    tail_tiles = x_tail.shape[0] // tm
    return pl.pallas_call(
        functools.partial(_norm_mod_body, head_tiles=head_tiles),
        grid=(head_tiles + tail_tiles,),
        in_specs=[
            pl.BlockSpec((tm, d), lambda i: (jnp.minimum(i, head_tiles - 1), 0)),
            pl.BlockSpec((tm, d), lambda i: (jnp.maximum(i - head_tiles, 0), 0)),
            _gain_spec(layer, 0, d),
            _mod_spec(layer, 0, d, tm, seq),
            _mod_spec(layer, 1, d, tm, seq),
        ],
        out_specs=pl.BlockSpec((tm, d), lambda i: (i, 0)),
        out_shape=jax.ShapeDtypeStruct(((head_tiles + tail_tiles) * tm, d), BF16),
        compiler_params=_params(("arbitrary",)),
        name="norm_mod",
    )(x_head, x_tail, gains4, mod4, mod4)


def _ffn_up_body(h_ref, wg_ref, wu_ref, o_ref):
    h = h_ref[...]
    a = _dot(h, wg_ref[...])
    b = _dot(h, wu_ref[...])
    o_ref[...] = (_silu(a) * b).astype(BF16)


def _ffn_up(h, w_gate, w_up, layer, which):
    t, d = h.shape
    n_tiles = w_gate.shape[-1] // TN_UP
    wspec = pl.BlockSpec((None, None, d, TN_UP), lambda i, j: (layer, which, 0, j))
    return pl.pallas_call(
        _ffn_up_body,
        grid=(t // TM_UP, n_tiles),
        in_specs=[pl.BlockSpec((TM_UP, d), lambda i, j: (i, 0)), wspec, wspec],
        out_specs=pl.BlockSpec((None, TM_UP, TN_UP), lambda i, j: (j, i, 0)),
        out_shape=jax.ShapeDtypeStruct((n_tiles, t, TN_UP), BF16),
        compiler_params=_params(("arbitrary", "arbitrary")),
        name="ffn_up",
    )(h, w_gate, w_up)


def _down_epi_body(a_ref, w_ref, x_hbm, xtail_hbm, gate_ref, gpost_ref, *rest, coef, nk, tm, with_next,
                   head_tiles):
    if with_next:
        (gnext_ref, shift_ref, scale_ref, xo_hbm, ho_hbm,
         acc_ref, stat_ref, xbuf_ref, sem_ref, hbuf_ref) = rest
    else:
        xo_hbm, acc_ref, stat_ref, xbuf_ref, sem_ref = rest
    i = pl.program_id(0)
    k = pl.program_id(1)
    n_tiles = pl.num_programs(0)
    d = acc_ref.shape[-1]
    sem_in, sem_xout, sem_hout = 0, 1, 2

    def tile_rows(tile):
        return pl.ds(pl.multiple_of(tile * tm, tm), tm)

    def residual_copy(src, tile):
        return pltpu.make_async_copy(src.at[tile_rows(tile), :], xbuf_ref, sem_ref.at[sem_in])

    def residual_read(tile, action):
        if head_tiles is None:
            action(residual_copy(x_hbm, tile))
        else:
            @pl.when(tile < head_tiles)
            def _():
                action(residual_copy(x_hbm, tile))

            @pl.when(tile >= head_tiles)
            def _():
                action(residual_copy(xtail_hbm, tile - head_tiles))

    def x_writeback(tile):
        return pltpu.make_async_copy(xbuf_ref, xo_hbm.at[tile_rows(tile), :], sem_ref.at[sem_xout])

    def h_writeback(tile):
        return pltpu.make_async_copy(hbuf_ref, ho_hbm.at[tile_rows(tile), :], sem_ref.at[sem_hout])

    @pl.when(k == 0)
    def _():
        acc_ref[...] = _dot(a_ref[...], w_ref[...])

    @pl.when(jnp.logical_and(k != 0, k != nk - 1))
    def _():
        acc_ref[...] += _dot(a_ref[...], w_ref[...])

    @pl.when(k == nk - 1)
    def _():
        y = acc_ref[...] + _dot(a_ref[...], w_ref[...])
        acc_ref[...] = y
        stat_ref[0, :tm, :] = jnp.broadcast_to(_rstd(y), (tm, stat_ref.shape[-1]))

    @pl.when(k == DOWN_READ_STEP)
    def _():
        @pl.when(i > 0)
        def _():
            x_writeback(i - 1).wait()

        residual_read(i, lambda copy: copy.start())

    @pl.when(k == nk - 1)
    def _():
        gate_gain = (coef * gate_ref[...]) * gpost_ref[...]
        if with_next:
            next_gain = gnext_ref[...] * (1.0 + scale_ref[...])
            shift = shift_ref[...]

        lanes = stat_ref.shape[-1]

        def chunk_rows(c):
            return pl.ds(pl.multiple_of(c * EPI_ROWS, EPI_ROWS), EPI_ROWS)

        def put_stat(slot, rows, val):
            stat_ref[slot, rows, :] = jnp.broadcast_to(val, (EPI_ROWS, lanes))

        def get_stat(slot, rows):
            return jnp.tile(stat_ref[slot, rows, :], (1, d // lanes))

        def pass_residual(c, carry):
            rows = chunk_rows(c)
            xn = xbuf_ref[rows, :] + (acc_ref[rows, :] * get_stat(0, rows)) * gate_gain
            xbuf_ref[rows, :] = xn
            if with_next:
                put_stat(1, rows, _rstd(xn))
            return carry

        def pass_modulate(c, carry):
            rows = chunk_rows(c)
            hn = (xbuf_ref[rows, :] * get_stat(1, rows)) * next_gain + shift
            hbuf_ref[rows, :] = hn.astype(BF16)
            return carry

        n_chunks = tm // EPI_ROWS
        residual_read(i, lambda copy: copy.wait())
        lax.fori_loop(0, n_chunks, pass_residual, 0, unroll=EPI_UNROLL)
        x_writeback(i).start()
        if with_next:
            @pl.when(i > 0)
            def _():
                h_writeback(i - 1).wait()

            lax.fori_loop(0, n_chunks, pass_modulate, 0, unroll=EPI_UNROLL)
            h_writeback(i).start()

        @pl.when(i == n_tiles - 1)
        def _():
            x_writeback(i).wait()
            if with_next:
                h_writeback(i).wait()


def _down_epi(a, w, wsel, xs, rows, seq, mod4, gains4, layer, gate_chunk, gpost_idx, coef,
              nxt=None):
    tm, tk = TM_DOWN, TK_DOWN
    if isinstance(xs, tuple):
        x_head, x_tail = xs
        assert x_head.shape[0] % tm == 0 and x_tail.shape[0] % tm == 0
        head_tiles = x_head.shape[0] // tm
    else:
        x_head, x_tail, head_tiles = xs, xs, None
    d = x_head.shape[1]
    if a.ndim == 3:
        assert a.shape[2] == tk
        nk = a.shape[0]
        aspec = pl.BlockSpec((None, tm, tk), lambda i, k: (k, i, 0))
    else:
        nk = a.shape[1] // tk
        aspec = pl.BlockSpec((tm, tk), lambda i, k: (i, k))
    assert nk - 1 > DOWN_READ_STEP and rows % tm == 0
    nlead = len(wsel)
    wspec = pl.BlockSpec((None,) * nlead + (tk, d), lambda i, k: tuple(wsel) + (k, 0))
    hbm = pl.BlockSpec(memory_space=pl.ANY)
    in_specs = [
        aspec,
        wspec,
        hbm,
        hbm,
        _mod_spec(layer, gate_chunk, d, tm, seq),
        _gain_spec(layer, gpost_idx, d),
    ]
    args = [a, w, x_head, x_tail, mod4, gains4]
    out_specs = [hbm]
    out_shape = [jax.ShapeDtypeStruct((rows, d), F32)]
    scratch = [
        pltpu.VMEM((tm, d), F32),
        pltpu.VMEM((2, tm + 8, V7X_LANES), F32),
        pltpu.VMEM((tm, d), F32),
        pltpu.SemaphoreType.DMA((3,)),
    ]
    if nxt is not None:
        mod_n, gains_n, layer_n, gain_idx, shift_chunk, scale_chunk = nxt
        in_specs += [
            _gain_spec(layer_n, gain_idx, d),
            _mod_spec(layer_n, shift_chunk, d, tm, seq),
            _mod_spec(layer_n, scale_chunk, d, tm, seq),
        ]
        args += [gains_n, mod_n, mod_n]
        out_specs.append(hbm)
        out_shape.append(jax.ShapeDtypeStruct((rows, d), BF16))
        scratch.append(pltpu.VMEM((tm, d), BF16))
    body = functools.partial(_down_epi_body, coef=coef, nk=nk, tm=tm, with_next=nxt is not None,
                             head_tiles=head_tiles)
    out = pl.pallas_call(
        body,
        grid=(rows // tm, nk),
        in_specs=in_specs,
        out_specs=out_specs,
        out_shape=out_shape,
        scratch_shapes=scratch,
        compiler_params=_params(("arbitrary", "arbitrary")),
        name="down_epi",
    )(*args)
    return (out[0], out[1]) if nxt is not None else (out[0], None)


def _in_tiles(j, lo, hi):
    return jnp.logical_and(j >= lo // TN_PROJ, j < hi // TN_PROJ)


def _proj_body(h_ref, w_ref, cos_ref, sin_ref, o_ref, *, n_lat_tiles):
    i = pl.program_id(0)
    j = pl.program_id(1)
    hd = HEAD_DIM
    acc = _dot(h_ref[...], w_ref[...].astype(BF16))
    is_q =_in_tiles(j, OFF_QA, OFF_KA) | _in_tiles(j, OFF_QB, OFF_KB) | _in_tiles(j, OFF_QC, OFF_KC)
    qscale = jnp.where(is_q, Q_SCALE, 1.0).astype(F32)
    latent = i < n_lat_tiles
    rope_full = latent & (_in_tiles(j, OFF_QB, OFF_KB) | _in_tiles(j, OFF_QC, OFF_VC))
    rope_half = latent & _in_tiles(j, OFF_KB, OFF_QC)

    def rope_heads(n_heads, scale=None):
        cos = cos_ref[...]
        sin = sin_ref[...]
        if scale is not None:
            cos = cos * scale
            sin = sin * scale
        lane = lax.broadcasted_iota(jnp.int32, cos.shape, 1)
        quarter = hd // 4
        first = (lane % (2 * quarter)) < quarter
        for hh in range(n_heads):
            a = acc[:, hh * hd:(hh + 1) * hd]
            partner = jnp.where(first, pltpu.roll(a, hd - quarter, 1), pltpu.roll(a, quarter, 1))
            o_ref[:, hh * hd:(hh + 1) * hd] = (a * cos + partner * sin).astype(BF16)

    @pl.when(rope_full)
    def _():
        rope_heads(TN_PROJ // hd, qscale)

    @pl.when(rope_half)
    def _():
        n_rope = (OFF_VB - OFF_KB) // hd
        rope_heads(n_rope)
        o_ref[:, n_rope * hd:] = acc[:, n_rope * hd:].astype(BF16)

    @pl.when(jnp.logical_not(rope_full | rope_half))
    def _():
        o_ref[...] = (acc * qscale).astype(BF16)


def _proj(h, w_in, layer, cos_t, sin_t, seq, t_lat):
    t, d = h.shape
    width = w_in.shape[-1]
    n_tiles = width // TN_PROJ
    pos_tiles = seq // TM_PROJ
    assert OFF_KB % TN_PROJ == 0 and OFF_QC - OFF_KB == TN_PROJ
    tspec = pl.BlockSpec((TM_PROJ, HEAD_DIM), lambda i, j: (i % pos_tiles, 0))
    return pl.pallas_call(
        functools.partial(_proj_body, n_lat_tiles=t_lat // TM_PROJ),
        grid=(t // TM_PROJ, n_tiles),
        in_specs=[
            pl.BlockSpec((TM_PROJ, d), lambda i, j: (i, 0)),
            pl.BlockSpec((None, d, TN_PROJ), lambda i, j: (layer, 0, j)),
            tspec, tspec,
        ],
        out_specs=pl.BlockSpec((TM_PROJ, TN_PROJ), lambda i, j: (i, j)),
        out_shape=jax.ShapeDtypeStruct((t, width), BF16),
        compiler_params=_params(("arbitrary", "arbitrary")),
        name="proj_rope",
    )(h, w_in, cos_t, sin_t)


def _rope_tables(n):
    t = jnp.arange(n, dtype=jnp.int32)
    row = (t // GRID_W).astype(F32)
    col = (t % GRID_W).astype(F32)
    half = HEAD_DIM // 2
    inv = ROPE_BASE ** (-jnp.arange(0, half, 2, dtype=F32) / half)
    ang_r = row[:, None] * inv
    ang_c = col[:, None] * inv
    cos_h = jnp.concatenate([jnp.cos(ang_r)] * 2 + [jnp.cos(ang_c)] * 2, axis=-1)
    sin_h = jnp.concatenate([-jnp.sin(ang_r), jnp.sin(ang_r), -jnp.sin(ang_c), jnp.sin(ang_c)], axis=-1)
    return cos_h, sin_h


def _na_key_row_start(blk, rows):
    return np.clip(blk * NA_Q_ROWS - NA_WIN_ROWS // 2, 0, rows - NA_KEY_ROWS)


def _na_bias_table(rpb, rows):
    kh, kw = NA_WIN_ROWS, NA_WIN_COLS
    nblk = rows // NA_Q_ROWS
    assert rows % NA_Q_ROWS == 0 and nblk >= 3 and rows >= NA_KEY_ROWS
    qc = np.arange(GRID_W)
    kc = np.arange(GRID_W)
    col_start = np.clip(qc - kw // 2, 0, GRID_W - kw)
    col_ok = (kc[None, :] >= col_start[:, None]) & (kc[None, :] < col_start[:, None] + kw)
    dc = kc[None, :] - qc[:, None] + NA_WIN_COLS - 1
    col_hot = (dc[None] == np.arange(2 * NA_WIN_COLS - 1)[:, None, None]).astype(np.float32)
    n_dr = 2 * NA_WIN_ROWS - 1
    row_hot = np.zeros((3, NA_Q_ROWS, NA_KEY_ROWS, n_dr), np.float32)
    for cls, blk in enumerate((0, 1, nblk - 1)):
        ks = _na_key_row_start(blk, rows)
        for j in range(NA_Q_ROWS):
            r = blk * NA_Q_ROWS + j
            rs = np.clip(r - kh // 2, 0, rows - kh)
            for i in range(NA_KEY_ROWS):
                if rs <= ks + i < rs + kh:
                    row_hot[cls, j, i, ks + i - r + NA_WIN_ROWS - 1] = 1.0
    b = jnp.einsum('cjir,hrd,dqk->hcjqik', row_hot, rpb.astype(F32) * LOG2E, col_hot,
                   precision=lax.Precision.HIGHEST)
    ok = (row_hot.sum(-1) > 0)[:, :, None, :, None] & col_ok[None, None, :, None, :]
    b = jnp.where(ok[None], b, NEG_INF)
    return b.reshape(rpb.shape[0], 3, NA_Q_ROWS * GRID_W, NA_KEY_ROWS * GRID_W)


def _softmax_parts(parts, extra=None):
    m = functools.reduce(jnp.maximum, [jnp.max(p, axis=-1, keepdims=True) for p in parts])
    if extra is not None:
        m = jnp.maximum(m, extra)
    es = [jnp.exp2(p - m) for p in parts]
    l = functools.reduce(jnp.add, [jnp.sum(e, axis=-1, keepdims=True) for e in es])
    if extra is not None:
        l = l + jnp.exp2(extra - m)
    inv = 1.0 / l
    return [e * inv for e in es]


def _na_body(q_ref, k_ref, v_ref, kx_ref, vx_ref, bias_ref, att_in_ref, o_ref, *, rows):
    del att_in_ref
    nblk = rows // NA_Q_ROWS
    nq = NA_Q_ROWS * GRID_W
    nkeys = NA_KEY_ROWS * GRID_W
    kx = kx_ref[...]
    vx = vx_ref[...]

    def blk_fn(bi, carry):
        ks = jnp.clip(bi * NA_Q_ROWS - NA_WIN_ROWS // 2, 0, rows - NA_KEY_ROWS)
        cls = jnp.where(bi == 0, 0, jnp.where(bi == nblk - 1, 2, 1))
        qrows = pl.ds(pl.multiple_of(bi * nq, nq), nq)
        krows = pl.ds(pl.multiple_of(ks * GRID_W, GRID_W), nkeys)
        q = q_ref[qrows, :]
        s_loc = _dot_t(q, k_ref[krows, :]) + bias_ref[cls]
        s_ctx = _dot_t(q, kx)
        p_loc, p_ctx = _softmax_parts([s_loc, s_ctx])
        o = _dot(p_loc.astype(BF16), v_ref[krows, :]) + _dot(p_ctx.astype(BF16), vx)
        o_ref[qrows, :] = o.astype(BF16)
        return carry

    lax.fori_loop(0, nblk, blk_fn, 0, unroll=NA_UNROLL)


def _na(qkv, att, bias_tbl, batch, seq, ctx_len):
    rows = seq // GRID_W
    hd = HEAD_DIM
    ctx_blk0 = batch * seq // ctx_len

    def col(off):
        return off // hd

    return pl.pallas_call(
        functools.partial(_na_body, rows=rows),
        grid=(batch, NA_HEADS),
        in_specs=[
            pl.BlockSpec((seq, hd), lambda b, h: (b, col(OFF_QA) + h)),
            pl.BlockSpec((seq, hd), lambda b, h: (b, col(OFF_KA) + h)),
            pl.BlockSpec((seq, hd), lambda b, h: (b, col(OFF_VA) + h)),
            pl.BlockSpec((ctx_len, hd), lambda b, h: (ctx_blk0 + b, col(OFF_KA) + h)),
            pl.BlockSpec((ctx_len, hd), lambda b, h: (ctx_blk0 + b, col(OFF_VA) + h)),
            pl.BlockSpec((None,) + bias_tbl.shape[1:], lambda b, h: (h, 0, 0, 0)),
            pl.BlockSpec(memory_space=pl.ANY),
        ],
        out_specs=pl.BlockSpec((seq, hd), lambda b, h: (b, col(OFF_OA) + h)),
        out_shape=jax.ShapeDtypeStruct(att.shape, att.dtype),
        input_output_aliases={6: 0},
        compiler_params=_params(("arbitrary", "arbitrary")),
        name="na_attn",
    )(qkv, qkv, qkv, qkv, qkv, bias_tbl, att)


def _sw_body(sink_ref, q_ref, k_ref, v_ref, kx_ref, vx_ref, att_in_ref, o_ref, *, seq):
    del att_in_ref
    kv = pl.program_id(1)
    blk = SW_BLOCK
    span = 3 * blk
    g = SW_GROUP
    kx = kx_ref[...]
    vx = vx_ref[...]
    grp = lax.broadcasted_iota(jnp.int32, (g * blk, 1), 0) // blk
    sink_col = jnp.zeros((g * blk, 1), F32)
    for gi in range(g):
        sink_col = jnp.where(grp == gi, sink_ref[kv * g + gi] * LOG2E, sink_col)
    qoff = lax.broadcasted_iota(jnp.int32, (g * blk, span), 0) % blk
    koff = lax.broadcasted_iota(jnp.int32, (g * blk, span), 1)

    def blk_fn(n, carry):
        start = jnp.clip((n - 1) * blk, 0, seq - span)
        qrows = pl.ds(pl.multiple_of(n * blk, blk), blk)
        krows = pl.ds(pl.multiple_of(start, blk), span)
        qb = q_ref[qrows, :]
        qs = jnp.concatenate([qb[:, gi * HEAD_DIM:(gi + 1) * HEAD_DIM] for gi in range(g)], axis=0)
        s_loc = _dot_t(qs, k_ref[krows, :])
        dist = (n * blk + qoff) - (start + koff)
        s_loc = jnp.where(jnp.abs(dist) <= SW_WINDOW, s_loc, NEG_INF)
        s_ctx = _dot_t(qs, kx)
        p_loc, p_ctx = _softmax_parts([s_loc, s_ctx], extra=sink_col)
        o = _dot(p_loc.astype(BF16), v_ref[krows, :]) + _dot(p_ctx.astype(BF16), vx)
        for gi in range(g):
            o_ref[qrows, gi * HEAD_DIM:(gi + 1) * HEAD_DIM] = o[gi * blk:(gi + 1) * blk].astype(BF16)
        return carry

    lax.fori_loop(0, seq // blk, blk_fn, 0, unroll=SW_UNROLL)


def _sw(qkv, att, sink, batch, seq, ctx_len):
    hd = HEAD_DIM
    gw = SW_GROUP * hd
    ctx_blk0 = batch * seq // ctx_len
    return pl.pallas_call(
        functools.partial(_sw_body, seq=seq),
        grid=(batch, SW_KV_HEADS),
        in_specs=[
            pl.BlockSpec(memory_space=pltpu.SMEM),
            pl.BlockSpec((seq, gw), lambda b, k: (b, OFF_QB // gw + k)),
            pl.BlockSpec((seq, hd), lambda b, k: (b, OFF_KB // hd + k)),
            pl.BlockSpec((seq, hd), lambda b, k: (b, OFF_VB // hd + k)),
            pl.BlockSpec((ctx_len, hd), lambda b, k: (ctx_blk0 + b, OFF_KB // hd + k)),
            pl.BlockSpec((ctx_len, hd), lambda b, k: (ctx_blk0 + b, OFF_VB // hd + k)),
            pl.BlockSpec(memory_space=pl.ANY),
        ],
        out_specs=pl.BlockSpec((seq, gw), lambda b, k: (b, OFF_OB // gw + k)),
        out_shape=jax.ShapeDtypeStruct(att.shape, att.dtype),
        input_output_aliases={6: 0},
        compiler_params=_params(("arbitrary", "arbitrary")),
        name="sw_attn",
    )(sink, qkv, qkv, qkv, qkv, qkv, att)


def _df_lambda(lam_ref, lambda_init):
    lv = lam_ref[...]
    s01 = jnp.sum(lv[0:1] * lv[1:2], axis=-1, keepdims=True)
    s23 = jnp.sum(lv[2:3] * lv[3:4], axis=-1, keepdims=True)
    return jnp.exp(s01) - jnp.exp(s23) + lambda_init


def _df_core(q, k_parts, v_parts, lam, subln, lambda_init):
    hd = HEAD_DIM
    exps, weights = [], []
    for c in range(2):
        qc = q[:, c * hd:(c + 1) * hd]
        scores = [_dot_t(qc, kp[:, c * hd:(c + 1) * hd]) for kp in k_parts]
        m = functools.reduce(jnp.maximum, [jnp.max(s, axis=-1, keepdims=True) for s in scores])
        es = [jnp.exp2(s - m) for s in scores]
        l = functools.reduce(jnp.add, [jnp.sum(e, axis=-1, keepdims=True) for e in es])
        exps.append(es)
        weights.append(1.0 / l)
    w0 = weights[0]
    w1 = lam * weights[1]
    o = None
    for i, vp in enumerate(v_parts):
        a = (exps[0][i] * w0 - exps[1][i] * w1).astype(BF16)
        term = _dot(a, vp[...])
        o = term if o is None else o + term
    return o * _rstd(o) * subln * (1.0 - lambda_init)


def _df_body(lam_ref, subln_ref, q_ref, k_ref, v_ref, kx_ref, vx_ref, att_in_ref, o_ref, *,
             lambda_init):
    del att_in_ref
    lam = _df_lambda(lam_ref, lambda_init)
    subln = subln_ref[...]
    for sub in range(DF_Q_STEP // DF_Q_BLOCK):
        rows = slice(sub * DF_Q_BLOCK, (sub + 1) * DF_Q_BLOCK)
        o = _df_core(q_ref[rows, :], [k_ref, kx_ref], [v_ref, vx_ref], lam, subln, lambda_init)
        o_ref[rows, :] = o.astype(BF16)


def _df(qkv, att, lam4, subln4, layer, lambda_init, batch, seq, ctx_len):
    w = 2 * HEAD_DIM
    nqb = seq // DF_Q_STEP
    ctx_blk0 = batch * seq // ctx_len
    return pl.pallas_call(
        functools.partial(_df_body, lambda_init=lambda_init),
        grid=(batch, DF_HEADS, nqb),
        in_specs=[
            pl.BlockSpec((None, 4, HEAD_DIM), lambda b, h, n: (layer, 0, 0)),
            pl.BlockSpec((None, 1, w), lambda b, h, n: (layer, 0, 0)),
            pl.BlockSpec((DF_Q_STEP, w), lambda b, h, n: (b * nqb + n, OFF_QC // w + h)),
            pl.BlockSpec((seq, w), lambda b, h, n: (b, OFF_KC // w + h)),
            pl.BlockSpec((seq, w), lambda b, h, n: (b, OFF_VC // w + h)),
            pl.BlockSpec((ctx_len, w), lambda b, h, n: (ctx_blk0 + b, OFF_KC // w + h)),
            pl.BlockSpec((ctx_len, w), lambda b, h, n: (ctx_blk0 + b, OFF_VC // w + h)),
            pl.BlockSpec(memory_space=pl.ANY),
        ],
        out_specs=pl.BlockSpec((DF_Q_STEP, w), lambda b, h, n: (b * nqb + n, OFF_OC // w + h)),
        out_shape=jax.ShapeDtypeStruct(att.shape, att.dtype),
        input_output_aliases={7: 0},
        compiler_params=_params(("arbitrary", "arbitrary", "arbitrary")),
        name="df_attn",
    )(lam4, subln4, qkv, qkv, qkv, qkv, qkv, att)


def _ctx_body(sink_ref, lam_ref, subln_ref, x_ref, att_in_ref, o_ref, *, lambda_init):
    del att_in_ref
    hd = HEAD_DIM

    def cols(off, width=hd):
        return x_ref[:, off:off + width]

    for h in range(NA_HEADS):
        s = _dot_t(cols(OFF_QA + h * hd), cols(OFF_KA + h * hd))
        (p,) = _softmax_parts([s])
        o = _dot(p.astype(BF16), cols(OFF_VA + h * hd))
        o_ref[:, OFF_OA + h * hd:OFF_OA + (h + 1) * hd] = o.astype(BF16)

    for hq in range(SW_HEADS):
        kvh = hq // SW_GROUP
        s = _dot_t(cols(OFF_QB + hq * hd), cols(OFF_KB + kvh * hd))
        sink = jnp.full((s.shape[0], 1), sink_ref[hq] * LOG2E, F32)
        (p,) = _softmax_parts([s], extra=sink)
        o = _dot(p.astype(BF16), cols(OFF_VB + kvh * hd))
        o_ref[:, OFF_OB + hq * hd:OFF_OB + (hq + 1) * hd] = o.astype(BF16)

    lam = _df_lambda(lam_ref, lambda_init)
    subln = subln_ref[...]
    for h in range(DF_HEADS):
        w = 2 * hd
        o = _df_core(cols(OFF_QC + h * w, w), [cols(OFF_KC + h * w, w)], [cols(OFF_VC + h * w, w)],
                     lam, subln, lambda_init)
        o_ref[:, OFF_OC + h * w:OFF_OC + (h + 1) * w] = o.astype(BF16)


def _ctx_attn(qkv, att, sink, lam4, subln4, layer, lambda_init, batch, seq, ctx_len):
    ctx_blk0 = batch * seq // ctx_len
    return pl.pallas_call(
        functools.partial(_ctx_body, lambda_init=lambda_init),
        grid=(batch,),
        in_specs=[
            pl.BlockSpec(memory_space=pltpu.SMEM),
            pl.BlockSpec((None, 4, HEAD_DIM), lambda b: (layer, 0, 0)),
            pl.BlockSpec((None, 1, 2 * HEAD_DIM), lambda b: (layer, 0, 0)),
            pl.BlockSpec((ctx_len, IN_WIDTH), lambda b: (ctx_blk0 + b, 0)),
            pl.BlockSpec(memory_space=pl.ANY),
        ],
        out_specs=pl.BlockSpec((ctx_len, ATT_WIDTH), lambda b: (ctx_blk0 + b, 0)),
        out_shape=jax.ShapeDtypeStruct(att.shape, att.dtype),
        input_output_aliases={4: 0},
        compiler_params=_params(("arbitrary",)),
        name="ctx_attn",
    )(sink, lam4, subln4, qkv, att)


def _merge_body(h_ref, att_ref, wg_ref, bg_ref, wb_ref, o_ref, *, branch_width):
    h = h_ref[...]
    acc = None
    for i in range(3):
        gate = jax.nn.sigmoid(_dot(h, wg_ref[i]) + bg_ref[i])
        br = _dot(att_ref[:, i * branch_width:(i + 1) * branch_width], wb_ref[i])
        acc = gate * br if acc is None else acc + gate * br
    o_ref[...] = acc.astype(BF16)


def _merge(h, att, w_gate, b_gate4, w_branch, layer, rows):
    d = h.shape[1]
    nb, bw = w_branch.shape[1], w_branch.shape[2]
    tm, tn = TM_MERGE, TN_MERGE
    return pl.pallas_call(
        functools.partial(_merge_body, branch_width=bw),
        grid=(rows // tm, d // tn),
        in_specs=[
            pl.BlockSpec((tm, d), lambda i, j: (i, 0)),
            pl.BlockSpec((tm, nb * bw), lambda i, j: (i, 0)),
            pl.BlockSpec((None, nb, d, tn), lambda i, j: (layer, 0, 0, j)),
            pl.BlockSpec((None, nb, 1, tn), lambda i, j: (layer, 0, 0, j)),
            pl.BlockSpec((None, nb, bw, tn), lambda i, j: (layer, 0, 0, j)),
        ],
        out_specs=pl.BlockSpec((tm, tn), lambda i, j: (i, j)),
        out_shape=jax.ShapeDtypeStruct((rows, d), BF16),
        compiler_params=_params(("arbitrary", "arbitrary")),
        name="merge",
    )(h, att, w_gate, b_gate4, w_branch)


@jax.jit
def _forward(x, c, ctx, c_ctx, w_ada, b_ada, norm_g, w_ffn_gate, w_ffn_up, w_ffn_down,
             w_in, na_rpb, sw_sink, df_lambda, df_subln_g, w_branch, w_gate, b_gate, w_out):
    batch, seq, d = x.shape
    ctx_len = ctx.shape[1]
    depth = w_ada.shape[0]
    t_lat = batch * seq
    t = t_lat + batch * ctx_len
    assert seq % TM_UP == 0 and (batch * ctx_len) % TM_UP == 0 and batch + 1 <= MOD_ROWS
    assert seq % ctx_len == 0 and ctx_len % SW_BLOCK == 0

    xs = (x.reshape(t_lat, d), ctx.reshape(batch * ctx_len, d))
    cc = jnp.zeros((MOD_ROWS, d), F32).at[:batch].set(c).at[batch].set(c_ctx)
    mod4 = _adaln(cc, w_ada, b_ada).reshape(depth, MOD_ROWS, 1, N_MOD * d)
    gains4 = norm_g.reshape(depth, norm_g.shape[1], 1, d)

    wg16 = w_ffn_gate.astype(BF16)
    wu16 = w_ffn_up.astype(BF16)
    wd16 = w_ffn_down.astype(BF16)
    wbr16 = w_branch.astype(BF16)
    wgt16 = w_gate.astype(BF16)
    wout16 = w_out.astype(BF16)
    b_gate4 = b_gate.reshape(depth, b_gate.shape[1], 1, d)
    lam4 = df_lambda.astype(F32)
    subln4 = df_subln_g.reshape(depth, 1, -1)
    cos_t, sin_t = _rope_tables(seq)

    h = _norm_mod(xs[0], xs[1], mod4, gains4, 0, seq)
    for l in range(depth):
        last = l == depth - 1
        lambda_init = 0.8 - 0.6 * math.exp(-0.3 * l)
        u = _ffn_up(h, wg16, wu16, l, 0)
        xs, h = _down_epi(u, wd16, (l, 0), xs, t, seq, mod4, gains4, l, 2, 1, 0.5,
                          nxt=(mod4, gains4, l, 2, 3, 4))
        qkv = _proj(h, w_in, l, cos_t, sin_t, seq, t_lat)
        rows = t_lat if last else t
        att = jnp.zeros((rows, ATT_WIDTH), BF16)
        att = _na(qkv, att, _na_bias_table(na_rpb[l], seq // GRID_W), batch, seq, ctx_len)
        att = _sw(qkv, att, sw_sink[l], batch, seq, ctx_len)
        att = _df(qkv, att, lam4, subln4, l, lambda_init, batch, seq, ctx_len)
        if not last:
            att = _ctx_attn(qkv, att, sw_sink[l], lam4, subln4, l, lambda_init, batch, seq, ctx_len)
        merged = _merge(h, att, wgt16, b_gate4, wbr16, l, rows)
        xs, h = _down_epi(merged, wout16, (l,), xs, rows, seq, mod4, gains4, l, 5, 3, 1.0,
                          nxt=(mod4, gains4, l, 4, 6, 7))
        u = _ffn_up(h, wg16, wu16, l, 1)
        nxt = None if last else (mod4, gains4, l + 1, 0, 0, 1)
        xs, h = _down_epi(u, wd16, (l, 1), xs, rows, seq, mod4, gains4, l, 8, 5, 0.5, nxt=nxt)
    return xs.reshape(batch, seq, d)


def kernel(x, c, ctx, c_ctx, w_ada, b_ada, norm_g, w_ffn_gate, w_ffn_up, w_ffn_down,
           w_in, na_rpb, sw_sink, df_lambda, df_subln_g, w_branch, w_gate, b_gate, w_out):
    return _forward(x, c, ctx, c_ctx, w_ada, b_ada, norm_g, w_ffn_gate, w_ffn_up, w_ffn_down,
                    w_in, na_rpb, sw_sink, df_lambda, df_subln_g, w_branch, w_gate, b_gate, w_out)
```

```python
import functools
import math

import numpy as np
import jax
import jax.numpy as jnp
from jax import lax
from jax.experimental import pallas as pl
from jax.experimental.pallas import tpu as pltpu

F32 = jnp.float32
BF16 = jnp.bfloat16

GRID_W = 64
HEAD_DIM = 128
NA_HEADS = 8
NA_WIN_ROWS = 8
NA_WIN_COLS = 16
SW_HEADS = 8
SW_KV_HEADS = 2
SW_GROUP = SW_HEADS // SW_KV_HEADS
SW_WINDOW = 128
SW_BLOCK = 128
DF_HEADS = 4
DF_Q_BLOCK = 128
N_MOD = 9
ROPE_BASE = 10000.0
EPS = 1e-6
NEG_INF = -1e30
LOG2E = math.log2(math.e)
Q_SCALE = HEAD_DIM ** -0.5 * LOG2E

OFF_QA, OFF_KA, OFF_VA = 0, 1024, 2048
OFF_QB, OFF_KB, OFF_VB = 3072, 4096, 4352
OFF_QC, OFF_KC, OFF_VC = 4608, 5632, 6656
IN_WIDTH = 7680
OFF_OA, OFF_OB, OFF_OC = 0, 1024, 2048
ATT_WIDTH = 3072

V7X_LANES = 128
MOD_ROWS = 8
V7X_VMEM_LIMIT = 58 * 1024 * 1024

TM_UP, TN_UP = 1024, 512
TM_DOWN, TK_DOWN = 1024, 512
DOWN_WB_PARTS = 4
TM_PROJ, TN_PROJ = 1024, 512
DF_Q_STEP = 8 * DF_Q_BLOCK
NA_Q_ROWS = 4
NA_KEY_ROWS = 12
NA_UNROLL = 4
SW_UNROLL = 4
TM_MERGE, TN_MERGE = 1024, 256
TN_ADA = 512
EPI_ROWS = 16
EPI_UNROLL = 8


def _params(sem):
    return pltpu.CompilerParams(dimension_semantics=sem, vmem_limit_bytes=V7X_VMEM_LIMIT)


def _dot(a, b):
    return jnp.dot(a, b, preferred_element_type=F32)


def _dot_t(a, b):
    return lax.dot_general(a, b, (((1,), (1,)), ((), ())), preferred_element_type=F32)


def _rstd(y):
    return lax.rsqrt(jnp.mean(y * y, axis=-1, keepdims=True) + EPS)


def _silu(a):
    return a * jax.nn.sigmoid(a)


def _adaln_body(c_ref, w_ref, b_ref, o_ref):
    s = _silu(c_ref[...]).astype(BF16)
    o_ref[...] = _dot(s, w_ref[...].astype(BF16)) + b_ref[...]


def _adaln(cc, w_ada, b_ada):
    depth, d, nd = w_ada.shape
    return pl.pallas_call(
        _adaln_body,
        grid=(depth, nd // TN_ADA),
        in_specs=[
            pl.BlockSpec((MOD_ROWS, d), lambda l, j: (0, 0)),
            pl.BlockSpec((None, d, TN_ADA), lambda l, j: (l, 0, j)),
            pl.BlockSpec((None, 1, TN_ADA), lambda l, j: (l, 0, j)),
        ],
        out_specs=pl.BlockSpec((None, MOD_ROWS, TN_ADA), lambda l, j: (l, 0, j)),
        out_shape=jax.ShapeDtypeStruct((depth, MOD_ROWS, nd), F32),
        compiler_params=_params(("arbitrary", "arbitrary")),
        name="adaln",
    )(cc, w_ada, b_ada.reshape(depth, 1, nd))


def _mod_spec(layer, chunk, d, tm, seq, tile_lag=0):
    return pl.BlockSpec((None, None, 1, d),
                        lambda i, *_: (layer, (jnp.maximum(i - tile_lag, 0) * tm) // seq, 0, chunk))


def _gain_spec(layer, idx, d):
    return pl.BlockSpec((None, None, 1, d), lambda i, *_: (layer, idx, 0, 0))


def _norm_mod_body(x_ref, xtail_ref, g_ref, shift_ref, scale_ref, o_ref, *, head_tiles):
    def emit(src_ref):
        x = src_ref[...]
        y = x * _rstd(x) * g_ref[...]
        o_ref[...] = (y * (1.0 + scale_ref[...]) + shift_ref[...]).astype(BF16)

    i = pl.program_id(0)

    @pl.when(i < head_tiles)
    def _():
        emit(x_ref)

    @pl.when(i >= head_tiles)
    def _():
        emit(xtail_ref)


def _norm_mod(x_head, x_tail, mod4, gains4, layer, seq):
    d = x_head.shape[1]
    tm = 256
    head_tiles = x_head.shape[0] // tm
    tail_tiles = x_tail.shape[0] // tm
    return pl.pallas_call(
        functools.partial(_norm_mod_body, head_tiles=head_tiles),
        grid=(head_tiles + tail_tiles,),
        in_specs=[
            pl.BlockSpec((tm, d), lambda i: (jnp.minimum(i, head_tiles - 1), 0)),
            pl.BlockSpec((tm, d), lambda i: (jnp.maximum(i - head_tiles, 0), 0)),
            _gain_spec(layer, 0, d),
            _mod_spec(layer, 0, d, tm, seq),
            _mod_spec(layer, 1, d, tm, seq),
        ],
        out_specs=pl.BlockSpec((tm, d), lambda i: (i, 0)),
        out_shape=jax.ShapeDtypeStruct(((head_tiles + tail_tiles) * tm, d), BF16),
        compiler_params=_params(("arbitrary",)),
        name="norm_mod",
    )(x_head, x_tail, gains4, mod4, mod4)


def _ffn_up_body(h_ref, wg_ref, wu_ref, o_ref):
    h = h_ref[...]
    a = _dot(h, wg_ref[...])
    b = _dot(h, wu_ref[...])
    o_ref[...] = (_silu(a) * b).astype(BF16)


def _ffn_up(h, w_gate, w_up, layer, which):
    t, d = h.shape
    n_tiles = w_gate.shape[-1] // TN_UP
    wspec = pl.BlockSpec((None, None, d, TN_UP), lambda i, j: (layer, which, 0, j))
    return pl.pallas_call(
        _ffn_up_body,
        grid=(t // TM_UP, n_tiles),
        in_specs=[pl.BlockSpec((TM_UP, d), lambda i, j: (i, 0)), wspec, wspec],
        out_specs=pl.BlockSpec((None, TM_UP, TN_UP), lambda i, j: (j, i, 0)),
        out_shape=jax.ShapeDtypeStruct((n_tiles, t, TN_UP), BF16),
        compiler_params=_params(("arbitrary", "arbitrary")),
        name="ffn_up",
    )(h, w_gate, w_up)


def _down_epi_body(a_ref, w_ref, x_hbm, xtail_hbm, gate_ref, gpost_ref, *rest, coef, nk, tm, with_next,
                   head_tiles):
    if with_next:
        (gnext_ref, shift_ref, scale_ref, pshift_ref, pscale_ref, xo_hbm, ho_hbm,
         acc_ref, stat_ref, xbuf_ref, sem_ref, hbuf_ref) = rest
    else:
        xo_hbm, acc_ref, stat_ref, xbuf_ref, sem_ref = rest
    i = pl.program_id(0)
    k = pl.program_id(1)
    n_tiles = pl.num_programs(0)
    d = acc_ref.shape[-1]
    n_parts = DOWN_WB_PARTS
    part = tm // n_parts
    sem_in, sem_xout, sem_hout = 0, 1, 1 + n_parts

    def tile_rows(tile):
        return pl.ds(pl.multiple_of(tile * tm, tm), tm)

    def residual_copy(src, tile):
        return pltpu.make_async_copy(src.at[tile_rows(tile), :], xbuf_ref, sem_ref.at[sem_in])

    def residual_read(tile, action):
        if head_tiles is None:
            action(residual_copy(x_hbm, tile))
        else:
            @pl.when(tile < head_tiles)
            def _():
                action(residual_copy(x_hbm, tile))

            @pl.when(tile >= head_tiles)
            def _():
                action(residual_copy(xtail_hbm, tile - head_tiles))

    def writeback_copies(tile, p):
        src_rows = pl.ds(p * part, part)
        dst_rows = pl.ds(pl.multiple_of(tile * tm + p * part, part), part)
        copies = [pltpu.make_async_copy(xbuf_ref.at[src_rows, :], xo_hbm.at[dst_rows, :],
                                        sem_ref.at[sem_xout + p])]
        if with_next:
            copies.append(pltpu.make_async_copy(hbuf_ref.at[src_rows, :], ho_hbm.at[dst_rows, :],
                                                sem_ref.at[sem_hout + p]))
        return copies

    def start_writeback(tile, p):
        for copy in writeback_copies(tile, p):
            copy.start()

    def wait_writeback(tile):
        for p in range(n_parts):
            for copy in writeback_copies(tile, p):
                copy.wait()

    def partial_product():
        if len(a_ref.shape) == 3:
            a = jnp.concatenate([a_ref[j] for j in range(a_ref.shape[0])], axis=1)
        else:
            a = a_ref[...]
        return _dot(a, w_ref[...])

    lanes = stat_ref.shape[-1]
    n_chunks = tm // EPI_ROWS
    part_chunks = n_chunks // n_parts

    def chunk_rows(c):
        return pl.ds(pl.multiple_of(c * EPI_ROWS, EPI_ROWS), EPI_ROWS)

    def put_stat(slot, rows, val):
        stat_ref[slot, rows, :] = jnp.broadcast_to(val, (EPI_ROWS, lanes))

    def get_stat(slot, rows):
        return jnp.tile(stat_ref[slot, rows, :], (1, d // lanes))

    def modulate_chunks(first, count, shift_r, scale_r):
        next_gain = gnext_ref[...] * (1.0 + scale_r[...])
        shift = shift_r[...]
        for c in range(count):
            rows = chunk_rows(first + c)
            hn = (xbuf_ref[rows, :] * get_stat(1, rows)) * next_gain + shift
            hbuf_ref[rows, :] = hn.astype(BF16)

    finishing = jnp.logical_and(jnp.logical_and(k >= 1, k <= n_parts), i > 0)

    @pl.when(k == 0)
    def _():
        acc_ref[...] = partial_product()

    @pl.when(finishing)
    def _():
        acc_ref[...] += partial_product()
        if with_next:
            modulate_chunks((k - 1) * part_chunks, part_chunks, pshift_ref, pscale_ref)

    @pl.when(jnp.logical_and(jnp.logical_and(k != 0, k != nk - 1), jnp.logical_not(finishing)))
    def _():
        acc_ref[...] += partial_product()

    @pl.when(k == nk - 1)
    def _():
        y = acc_ref[...] + partial_product()
        acc_ref[...] = y
        stat_ref[0] = jnp.broadcast_to(_rstd(y), (tm, lanes))

    for p in range(n_parts):
        @pl.when(jnp.logical_and(k == p + 1, i > 0))
        def _(p=p):
            start_writeback(i - 1, p)

    @pl.when(k == n_parts + 1)
    def _():
        @pl.when(i > 0)
        def _():
            wait_writeback(i - 1)

        residual_read(i, lambda copy: copy.start())

    @pl.when(k == nk - 1)
    def _():
        gate_gain = (coef * gate_ref[...]) * gpost_ref[...]

        def pass_residual(c, carry):
            rows = chunk_rows(c)
            xn = xbuf_ref[rows, :] + (acc_ref[rows, :] * get_stat(0, rows)) * gate_gain
            xbuf_ref[rows, :] = xn
            if with_next:
                put_stat(1, rows, _rstd(xn))
            return carry

        residual_read(i, lambda copy: copy.wait())
        lax.fori_loop(0, n_chunks, pass_residual, 0, unroll=EPI_UNROLL)

        @pl.when(i == n_tiles - 1)
        def _():
            if with_next:
                def pass_modulate(g, carry):
                    modulate_chunks(g * EPI_UNROLL, EPI_UNROLL, shift_ref, scale_ref)
                    return carry

                lax.fori_loop(0, n_chunks // EPI_UNROLL, pass_modulate, 0)
            for p in range(n_parts):
                start_writeback(i, p)
            wait_writeback(i)


def _down_epi(a, w, wsel, xs, rows, seq, mod4, gains4, layer, gate_chunk, gpost_idx, coef,
              nxt=None):
    tm, tk = TM_DOWN, TK_DOWN
    if isinstance(xs, tuple):
        x_head, x_tail = xs
        assert x_head.shape[0] % tm == 0 and x_tail.shape[0] % tm == 0
        head_tiles = x_head.shape[0] // tm
    else:
        x_head, x_tail, head_tiles = xs, xs, None
    d = x_head.shape[1]
    if a.ndim == 3:
        per_step = tk // a.shape[2]
        assert per_step * a.shape[2] == tk and a.shape[0] % per_step == 0
        nk = a.shape[0] // per_step
        aspec = pl.BlockSpec((per_step, tm, a.shape[2]), lambda i, k: (k, i, 0))
    else:
        nk = a.shape[1] // tk
        aspec = pl.BlockSpec((tm, tk), lambda i, k: (i, k))
    assert nk - 1 > DOWN_WB_PARTS + 1 and rows % tm == 0
    assert tm % (DOWN_WB_PARTS * EPI_ROWS * EPI_UNROLL) == 0
    nlead = len(wsel)
    wspec = pl.BlockSpec((None,) * nlead + (tk, d), lambda i, k: tuple(wsel) + (k, 0))
    hbm = pl.BlockSpec(memory_space=pl.ANY)
    in_specs = [
        aspec,
        wspec,
        hbm,
        hbm,
        _mod_spec(layer, gate_chunk, d, tm, seq),
        _gain_spec(layer, gpost_idx, d),
    ]
    args = [a, w, x_head, x_tail, mod4, gains4]
    out_specs = [hbm]
    out_shape = [jax.ShapeDtypeStruct((rows, d), F32)]
    scratch = [
        pltpu.VMEM((tm, d), F32),
        pltpu.VMEM((2, tm, V7X_LANES), F32),
        pltpu.VMEM((tm, d), F32),
        pltpu.SemaphoreType.DMA((1 + 2 * DOWN_WB_PARTS,)),
    ]
    if nxt is not None:
        mod_n, gains_n, layer_n, gain_idx, shift_chunk, scale_chunk = nxt
        in_specs += [
            _gain_spec(layer_n, gain_idx, d),
            _mod_spec(layer_n, shift_chunk, d, tm, seq),
            _mod_spec(layer_n, scale_chunk, d, tm, seq),
            _mod_spec(layer_n, shift_chunk, d, tm, seq, tile_lag=1),
            _mod_spec(layer_n, scale_chunk, d, tm, seq, tile_lag=1),
        ]
        args += [gains_n, mod_n, mod_n, mod_n, mod_n]
        out_specs.append(hbm)
        out_shape.append(jax.ShapeDtypeStruct((rows, d), BF16))
        scratch.append(pltpu.VMEM((tm, d), BF16))
    body = functools.partial(_down_epi_body, coef=coef, nk=nk, tm=tm, with_next=nxt is not None,
                             head_tiles=head_tiles)
    out = pl.pallas_call(
        body,
        grid=(rows // tm, nk),
        in_specs=in_specs,
        out_specs=out_specs,
        out_shape=out_shape,
        scratch_shapes=scratch,
        compiler_params=_params(("arbitrary", "arbitrary")),
        name="down_epi",
    )(*args)
    return (out[0], out[1]) if nxt is not None else (out[0], None)


def _in_tiles(j, lo, hi):
    return jnp.logical_and(j >= lo // TN_PROJ, j < hi // TN_PROJ)


def _proj_body(h_ref, w_ref, cos_ref, sin_ref, o_ref, *, n_lat_tiles):
    i = pl.program_id(0)
    j = pl.program_id(1)
    hd = HEAD_DIM
    acc = _dot(h_ref[...], w_ref[...].astype(BF16))
    is_q =_in_tiles(j, OFF_QA, OFF_KA) | _in_tiles(j, OFF_QB, OFF_KB) | _in_tiles(j, OFF_QC, OFF_KC)
    qscale = jnp.where(is_q, Q_SCALE, 1.0).astype(F32)
    latent = i < n_lat_tiles
    rope_full = latent & (_in_tiles(j, OFF_QB, OFF_KB) | _in_tiles(j, OFF_QC, OFF_VC))
    rope_half = latent & _in_tiles(j, OFF_KB, OFF_QC)

    def rope_heads(n_heads, scale=None):
        cos = cos_ref[...]
        sin = sin_ref[...]
        if scale is not None:
            cos = cos * scale
            sin = sin * scale
        lane = lax.broadcasted_iota(jnp.int32, cos.shape, 1)
        quarter = hd // 4
        first = (lane % (2 * quarter)) < quarter
        for hh in range(n_heads):
            a = acc[:, hh * hd:(hh + 1) * hd]
            partner = jnp.where(first, pltpu.roll(a, hd - quarter, 1), pltpu.roll(a, quarter, 1))
            o_ref[:, hh * hd:(hh + 1) * hd] = (a * cos + partner * sin).astype(BF16)

    @pl.when(rope_full)
    def _():
        rope_heads(TN_PROJ // hd, qscale)

    @pl.when(rope_half)
    def _():
        n_rope = (OFF_VB - OFF_KB) // hd
        rope_heads(n_rope)
        o_ref[:, n_rope * hd:] = acc[:, n_rope * hd:].astype(BF16)

    @pl.when(jnp.logical_not(rope_full | rope_half))
    def _():
        o_ref[...] = (acc * qscale).astype(BF16)


def _proj(h, w_in, layer, cos_t, sin_t, seq, t_lat):
    t, d = h.shape
    width = w_in.shape[-1]
    n_tiles = width // TN_PROJ
    pos_tiles = seq // TM_PROJ
    assert OFF_KB % TN_PROJ == 0 and OFF_QC - OFF_KB == TN_PROJ
    tspec = pl.BlockSpec((TM_PROJ, HEAD_DIM), lambda i, j: (i % pos_tiles, 0))
    return pl.pallas_call(
        functools.partial(_proj_body, n_lat_tiles=t_lat // TM_PROJ),
        grid=(t // TM_PROJ, n_tiles),
        in_specs=[
            pl.BlockSpec((TM_PROJ, d), lambda i, j: (i, 0)),
            pl.BlockSpec((None, d, TN_PROJ), lambda i, j: (layer, 0, j)),
            tspec, tspec,
        ],
        out_specs=pl.BlockSpec((TM_PROJ, TN_PROJ), lambda i, j: (i, j)),
        out_shape=jax.ShapeDtypeStruct((t, width), BF16),
        compiler_params=_params(("arbitrary", "arbitrary")),
        name="proj_rope",
    )(h, w_in, cos_t, sin_t)


def _rope_tables(n):
    t = jnp.arange(n, dtype=jnp.int32)
    row = (t // GRID_W).astype(F32)
    col = (t % GRID_W).astype(F32)
    half = HEAD_DIM // 2
    inv = ROPE_BASE ** (-jnp.arange(0, half, 2, dtype=F32) / half)
    ang_r = row[:, None] * inv
    ang_c = col[:, None] * inv
    cos_h = jnp.concatenate([jnp.cos(ang_r)] * 2 + [jnp.cos(ang_c)] * 2, axis=-1)
    sin_h = jnp.concatenate([-jnp.sin(ang_r), jnp.sin(ang_r), -jnp.sin(ang_c), jnp.sin(ang_c)], axis=-1)
    return cos_h, sin_h


def _na_key_row_start(blk, rows):
    return np.clip(blk * NA_Q_ROWS - NA_WIN_ROWS // 2, 0, rows - NA_KEY_ROWS)


def _na_bias_table(rpb, rows):
    kh, kw = NA_WIN_ROWS, NA_WIN_COLS
    nblk = rows // NA_Q_ROWS
    assert rows % NA_Q_ROWS == 0 and nblk >= 3 and rows >= NA_KEY_ROWS
    qc = np.arange(GRID_W)
    kc = np.arange(GRID_W)
    col_start = np.clip(qc - kw // 2, 0, GRID_W - kw)
    col_ok = (kc[None, :] >= col_start[:, None]) & (kc[None, :] < col_start[:, None] + kw)
    dc = kc[None, :] - qc[:, None] + NA_WIN_COLS - 1
    col_hot = (dc[None] == np.arange(2 * NA_WIN_COLS - 1)[:, None, None]).astype(np.float32)
    n_dr = 2 * NA_WIN_ROWS - 1
    row_hot = np.zeros((3, NA_Q_ROWS, NA_KEY_ROWS, n_dr), np.float32)
    for cls, blk in enumerate((0, 1, nblk - 1)):
        ks = _na_key_row_start(blk, rows)
        for j in range(NA_Q_ROWS):
            r = blk * NA_Q_ROWS + j
            rs = np.clip(r - kh // 2, 0, rows - kh)
            for i in range(NA_KEY_ROWS):
                if rs <= ks + i < rs + kh:
                    row_hot[cls, j, i, ks + i - r + NA_WIN_ROWS - 1] = 1.0
    b = jnp.einsum('cjir,hrd,dqk->hcjqik', row_hot, rpb.astype(F32) * LOG2E, col_hot,
                   precision=lax.Precision.HIGHEST)
    ok = (row_hot.sum(-1) > 0)[:, :, None, :, None] & col_ok[None, None, :, None, :]
    b = jnp.where(ok[None], b, NEG_INF)
    return b.reshape(rpb.shape[0], 3, NA_Q_ROWS * GRID_W, NA_KEY_ROWS * GRID_W)


def _softmax_parts(parts, extra=None):
    m = functools.reduce(jnp.maximum, [jnp.max(p, axis=-1, keepdims=True) for p in parts])
    if extra is not None:
        m = jnp.maximum(m, extra)
    es = [jnp.exp2(p - m) for p in parts]
    l = functools.reduce(jnp.add, [jnp.sum(e, axis=-1, keepdims=True) for e in es])
    if extra is not None:
        l = l + jnp.exp2(extra - m)
    inv = 1.0 / l
    return [e * inv for e in es]


def _na_body(q_ref, k_ref, v_ref, kx_ref, vx_ref, bias_ref, att_in_ref, o_ref, *, rows):
    del att_in_ref
    nblk = rows // NA_Q_ROWS
    nq = NA_Q_ROWS * GRID_W
    nkeys = NA_KEY_ROWS * GRID_W
    kx = kx_ref[...]
    vx = vx_ref[...]

    def blk_fn(bi, carry):
        ks = jnp.clip(bi * NA_Q_ROWS - NA_WIN_ROWS // 2, 0, rows - NA_KEY_ROWS)
        cls = jnp.where(bi == 0, 0, jnp.where(bi == nblk - 1, 2, 1))
        qrows = pl.ds(pl.multiple_of(bi * nq, nq), nq)
        krows = pl.ds(pl.multiple_of(ks * GRID_W, GRID_W), nkeys)
        q = q_ref[qrows, :]
        s_loc = _dot_t(q, k_ref[krows, :]) + bias_ref[cls]
        s_ctx = _dot_t(q, kx)
        p_loc, p_ctx = _softmax_parts([s_loc, s_ctx])
        o = _dot(p_loc.astype(BF16), v_ref[krows, :]) + _dot(p_ctx.astype(BF16), vx)
        o_ref[qrows, :] = o.astype(BF16)
        return carry

    lax.fori_loop(0, nblk, blk_fn, 0, unroll=NA_UNROLL)


def _na(qkv, att, bias_tbl, batch, seq, ctx_len):
    rows = seq // GRID_W
    hd = HEAD_DIM
    ctx_blk0 = batch * seq // ctx_len

    def col(off):
        return off // hd

    return pl.pallas_call(
        functools.partial(_na_body, rows=rows),
        grid=(batch, NA_HEADS),
        in_specs=[
            pl.BlockSpec((seq, hd), lambda b, h: (b, col(OFF_QA) + h)),
            pl.BlockSpec((seq, hd), lambda b, h: (b, col(OFF_KA) + h)),
            pl.BlockSpec((seq, hd), lambda b, h: (b, col(OFF_VA) + h)),
            pl.BlockSpec((ctx_len, hd), lambda b, h: (ctx_blk0 + b, col(OFF_KA) + h)),
            pl.BlockSpec((ctx_len, hd), lambda b, h: (ctx_blk0 + b, col(OFF_VA) + h)),
            pl.BlockSpec((None,) + bias_tbl.shape[1:], lambda b, h: (h, 0, 0, 0)),
            pl.BlockSpec(memory_space=pl.ANY),
        ],
        out_specs=pl.BlockSpec((seq, hd), lambda b, h: (b, col(OFF_OA) + h)),
        out_shape=jax.ShapeDtypeStruct(att.shape, att.dtype),
        input_output_aliases={6: 0},
        compiler_params=_params(("arbitrary", "arbitrary")),
        name="na_attn",
    )(qkv, qkv, qkv, qkv, qkv, bias_tbl, att)


def _sw_body(sink_ref, q_ref, k_ref, v_ref, kx_ref, vx_ref, att_in_ref, o_ref, *, seq):
    del att_in_ref
    kv = pl.program_id(1)
    blk = SW_BLOCK
    span = 3 * blk
    g = SW_GROUP
    kx = kx_ref[...]
    vx = vx_ref[...]
    grp = lax.broadcasted_iota(jnp.int32, (g * blk, 1), 0) // blk
    sink_col = jnp.zeros((g * blk, 1), F32)
    for gi in range(g):
        sink_col = jnp.where(grp == gi, sink_ref[kv * g + gi] * LOG2E, sink_col)
    qoff = lax.broadcasted_iota(jnp.int32, (g * blk, span), 0) % blk
    koff = lax.broadcasted_iota(jnp.int32, (g * blk, span), 1)

    def blk_fn(n, carry):
        start = jnp.clip((n - 1) * blk, 0, seq - span)
        qrows = pl.ds(pl.multiple_of(n * blk, blk), blk)
        krows = pl.ds(pl.multiple_of(start, blk), span)
        qb = q_ref[qrows, :]
        qs = jnp.concatenate([qb[:, gi * HEAD_DIM:(gi + 1) * HEAD_DIM] for gi in range(g)], axis=0)
        s_loc = _dot_t(qs, k_ref[krows, :])
        dist = (n * blk + qoff) - (start + koff)
        s_loc = jnp.where(jnp.abs(dist) <= SW_WINDOW, s_loc, NEG_INF)
        s_ctx = _dot_t(qs, kx)
        p_loc, p_ctx = _softmax_parts([s_loc, s_ctx], extra=sink_col)
        o = _dot(p_loc.astype(BF16), v_ref[krows, :]) + _dot(p_ctx.astype(BF16), vx)
        for gi in range(g):
            o_ref[qrows, gi * HEAD_DIM:(gi + 1) * HEAD_DIM] = o[gi * blk:(gi + 1) * blk].astype(BF16)
        return carry

    lax.fori_loop(0, seq // blk, blk_fn, 0, unroll=SW_UNROLL)


def _sw(qkv, att, sink, batch, seq, ctx_len):
    hd = HEAD_DIM
    gw = SW_GROUP * hd
    ctx_blk0 = batch * seq // ctx_len
    return pl.pallas_call(
        functools.partial(_sw_body, seq=seq),
        grid=(batch, SW_KV_HEADS),
        in_specs=[
            pl.BlockSpec(memory_space=pltpu.SMEM),
            pl.BlockSpec((seq, gw), lambda b, k: (b, OFF_QB // gw + k)),
            pl.BlockSpec((seq, hd), lambda b, k: (b, OFF_KB // hd + k)),
            pl.BlockSpec((seq, hd), lambda b, k: (b, OFF_VB // hd + k)),
            pl.BlockSpec((ctx_len, hd), lambda b, k: (ctx_blk0 + b, OFF_KB // hd + k)),
            pl.BlockSpec((ctx_len, hd), lambda b, k: (ctx_blk0 + b, OFF_VB // hd + k)),
            pl.BlockSpec(memory_space=pl.ANY),
        ],
        out_specs=pl.BlockSpec((seq, gw), lambda b, k: (b, OFF_OB // gw + k)),
        out_shape=jax.ShapeDtypeStruct(att.shape, att.dtype),
        input_output_aliases={6: 0},
        compiler_params=_params(("arbitrary", "arbitrary")),
        name="sw_attn",
    )(sink, qkv, qkv, qkv, qkv, qkv, att)


def _df_lambda(lam_ref, lambda_init):
    lv = lam_ref[...]
    s01 = jnp.sum(lv[0:1] * lv[1:2], axis=-1, keepdims=True)
    s23 = jnp.sum(lv[2:3] * lv[3:4], axis=-1, keepdims=True)
    return jnp.exp(s01) - jnp.exp(s23) + lambda_init


def _df_core(q, k_parts, v_parts, lam, subln, lambda_init):
    hd = HEAD_DIM
    exps, weights = [], []
    for c in range(2):
        qc = q[:, c * hd:(c + 1) * hd]
        scores = [_dot_t(qc, kp[:, c * hd:(c + 1) * hd]) for kp in k_parts]
        m = functools.reduce(jnp.maximum, [jnp.max(s, axis=-1, keepdims=True) for s in scores])
        es = [jnp.exp2(s - m) for s in scores]
        l = functools.reduce(jnp.add, [jnp.sum(e, axis=-1, keepdims=True) for e in es])
        exps.append(es)
        weights.append(1.0 / l)
    w0 = weights[0]
    w1 = lam * weights[1]
    o = None
    for i, vp in enumerate(v_parts):
        a = (exps[0][i] * w0 - exps[1][i] * w1).astype(BF16)
        term = _dot(a, vp[...])
        o = term if o is None else o + term
    return o * _rstd(o) * subln * (1.0 - lambda_init)


def _df_body(lam_ref, subln_ref, q_ref, k_ref, v_ref, kx_ref, vx_ref, att_in_ref, o_ref, *,
             lambda_init):
    del att_in_ref
    lam = _df_lambda(lam_ref, lambda_init)
    subln = subln_ref[...]
    for sub in range(DF_Q_STEP // DF_Q_BLOCK):
        rows = slice(sub * DF_Q_BLOCK, (sub + 1) * DF_Q_BLOCK)
        o = _df_core(q_ref[rows, :], [k_ref, kx_ref], [v_ref, vx_ref], lam, subln, lambda_init)
        o_ref[rows, :] = o.astype(BF16)


def _df(qkv, att, lam4, subln4, layer, lambda_init, batch, seq, ctx_len):
    w = 2 * HEAD_DIM
    nqb = seq // DF_Q_STEP
    ctx_blk0 = batch * seq // ctx_len
    return pl.pallas_call(
        functools.partial(_df_body, lambda_init=lambda_init),
        grid=(batch, DF_HEADS, nqb),
        in_specs=[
            pl.BlockSpec((None, 4, HEAD_DIM), lambda b, h, n: (layer, 0, 0)),
            pl.BlockSpec((None, 1, w), lambda b, h, n: (layer, 0, 0)),
            pl.BlockSpec((DF_Q_STEP, w), lambda b, h, n: (b * nqb + n, OFF_QC // w + h)),
            pl.BlockSpec((seq, w), lambda b, h, n: (b, OFF_KC // w + h)),
            pl.BlockSpec((seq, w), lambda b, h, n: (b, OFF_VC // w + h)),
            pl.BlockSpec((ctx_len, w), lambda b, h, n: (ctx_blk0 + b, OFF_KC // w + h)),
            pl.BlockSpec((ctx_len, w), lambda b, h, n: (ctx_blk0 + b, OFF_VC // w + h)),
            pl.BlockSpec(memory_space=pl.ANY),
        ],
        out_specs=pl.BlockSpec((DF_Q_STEP, w), lambda b, h, n: (b * nqb + n, OFF_OC // w + h)),
        out_shape=jax.ShapeDtypeStruct(att.shape, att.dtype),
        input_output_aliases={7: 0},
        compiler_params=_params(("arbitrary", "arbitrary", "arbitrary")),
        name="df_attn",
    )(lam4, subln4, qkv, qkv, qkv, qkv, qkv, att)


def _ctx_body(sink_ref, lam_ref, subln_ref, x_ref, att_in_ref, o_ref, *, lambda_init):
    del att_in_ref
    hd = HEAD_DIM

    def cols(off, width=hd):
        return x_ref[:, off:off + width]

    for h in range(NA_HEADS):
        s = _dot_t(cols(OFF_QA + h * hd), cols(OFF_KA + h * hd))
        (p,) = _softmax_parts([s])
        o = _dot(p.astype(BF16), cols(OFF_VA + h * hd))
        o_ref[:, OFF_OA + h * hd:OFF_OA + (h + 1) * hd] = o.astype(BF16)

    for hq in range(SW_HEADS):
        kvh = hq // SW_GROUP
        s = _dot_t(cols(OFF_QB + hq * hd), cols(OFF_KB + kvh * hd))
        sink = jnp.full((s.shape[0], 1), sink_ref[hq] * LOG2E, F32)
        (p,) = _softmax_parts([s], extra=sink)
        o = _dot(p.astype(BF16), cols(OFF_VB + kvh * hd))
        o_ref[:, OFF_OB + hq * hd:OFF_OB + (hq + 1) * hd] = o.astype(BF16)

    lam = _df_lambda(lam_ref, lambda_init)
    subln = subln_ref[...]
    for h in range(DF_HEADS):
        w = 2 * hd
        o = _df_core(cols(OFF_QC + h * w, w), [cols(OFF_KC + h * w, w)], [cols(OFF_VC + h * w, w)],
                     lam, subln, lambda_init)
        o_ref[:, OFF_OC + h * w:OFF_OC + (h + 1) * w] = o.astype(BF16)


def _ctx_attn(qkv, att, sink, lam4, subln4, layer, lambda_init, batch, seq, ctx_len):
    ctx_blk0 = batch * seq // ctx_len
    return pl.pallas_call(
        functools.partial(_ctx_body, lambda_init=lambda_init),
        grid=(batch,),
        in_specs=[
            pl.BlockSpec(memory_space=pltpu.SMEM),
            pl.BlockSpec((None, 4, HEAD_DIM), lambda b: (layer, 0, 0)),
            pl.BlockSpec((None, 1, 2 * HEAD_DIM), lambda b: (layer, 0, 0)),
            pl.BlockSpec((ctx_len, IN_WIDTH), lambda b: (ctx_blk0 + b, 0)),
            pl.BlockSpec(memory_space=pl.ANY),
        ],
        out_specs=pl.BlockSpec((ctx_len, ATT_WIDTH), lambda b: (ctx_blk0 + b, 0)),
        out_shape=jax.ShapeDtypeStruct(att.shape, att.dtype),
        input_output_aliases={4: 0},
        compiler_params=_params(("arbitrary",)),
        name="ctx_attn",
    )(sink, lam4, subln4, qkv, att)


def _merge_body(h_ref, att_ref, wg_ref, bg_ref, wb_ref, o_ref, *, branch_width):
    h = h_ref[...]
    acc = None
    for i in range(3):
        gate = jax.nn.sigmoid(_dot(h, wg_ref[i]) + bg_ref[i])
        br = _dot(att_ref[:, i * branch_width:(i + 1) * branch_width], wb_ref[i])
        acc = gate * br if acc is None else acc + gate * br
    o_ref[...] = acc.astype(BF16)


def _merge(h, att, w_gate, b_gate4, w_branch, layer, rows):
    d = h.shape[1]
    nb, bw = w_branch.shape[1], w_branch.shape[2]
    tm, tn = TM_MERGE, TN_MERGE
    return pl.pallas_call(
        functools.partial(_merge_body, branch_width=bw),
        grid=(rows // tm, d // tn),
        in_specs=[
            pl.BlockSpec((tm, d), lambda i, j: (i, 0)),
            pl.BlockSpec((tm, nb * bw), lambda i, j: (i, 0)),
            pl.BlockSpec((None, nb, d, tn), lambda i, j: (layer, 0, 0, j)),
            pl.BlockSpec((None, nb, 1, tn), lambda i, j: (layer, 0, 0, j)),
            pl.BlockSpec((None, nb, bw, tn), lambda i, j: (layer, 0, 0, j)),
        ],
        out_specs=pl.BlockSpec((tm, tn), lambda i, j: (i, j)),
        out_shape=jax.ShapeDtypeStruct((rows, d), BF16),
        compiler_params=_params(("arbitrary", "arbitrary")),
        name="merge",
    )(h, att, w_gate, b_gate4, w_branch)


@jax.jit
def _forward(x, c, ctx, c_ctx, w_ada, b_ada, norm_g, w_ffn_gate, w_ffn_up, w_ffn_down,
             w_in, na_rpb, sw_sink, df_lambda, df_subln_g, w_branch, w_gate, b_gate, w_out):
    batch, seq, d = x.shape
    ctx_len = ctx.shape[1]
    depth = w_ada.shape[0]
    t_lat = batch * seq
    t = t_lat + batch * ctx_len
    assert seq % TM_UP == 0 and (batch * ctx_len) % TM_UP == 0 and batch + 1 <= MOD_ROWS
    assert seq % ctx_len == 0 and ctx_len % SW_BLOCK == 0

    xs = (x.reshape(t_lat, d), ctx.reshape(batch * ctx_len, d))
    cc = jnp.zeros((MOD_ROWS, d), F32).at[:batch].set(c).at[batch].set(c_ctx)
    mod4 = _adaln(cc, w_ada, b_ada).reshape(depth, MOD_ROWS, 1, N_MOD * d)
    gains4 = norm_g.reshape(depth, norm_g.shape[1], 1, d)

    wg16 = w_ffn_gate.astype(BF16)
    wu16 = w_ffn_up.astype(BF16)
    wd16 = w_ffn_down.astype(BF16)
    wbr16 = w_branch.astype(BF16)
    wgt16 = w_gate.astype(BF16)
    wout16 = w_out.astype(BF16)
    b_gate4 = b_gate.reshape(depth, b_gate.shape[1], 1, d)
    lam4 = df_lambda.astype(F32)
    subln4 = df_subln_g.reshape(depth, 1, -1)
    cos_t, sin_t = _rope_tables(seq)

    h = _norm_mod(xs[0], xs[1], mod4, gains4, 0, seq)
    for l in range(depth):
        last = l == depth - 1
        lambda_init = 0.8 - 0.6 * math.exp(-0.3 * l)
        u = _ffn_up(h, wg16, wu16, l, 0)
        xs, h = _down_epi(u, wd16, (l, 0), xs, t, seq, mod4, gains4, l, 2, 1, 0.5,
                          nxt=(mod4, gains4, l, 2, 3, 4))
        qkv = _proj(h, w_in, l, cos_t, sin_t, seq, t_lat)
        rows = t_lat if last else t
        att = jnp.zeros((rows, ATT_WIDTH), BF16)
        att = _na(qkv, att, _na_bias_table(na_rpb[l], seq // GRID_W), batch, seq, ctx_len)
        att = _sw(qkv, att, sw_sink[l], batch, seq, ctx_len)
        att = _df(qkv, att, lam4, subln4, l, lambda_init, batch, seq, ctx_len)
        if not last:
            att = _ctx_attn(qkv, att, sw_sink[l], lam4, subln4, l, lambda_init, batch, seq, ctx_len)
        merged = _merge(h, att, wgt16, b_gate4, wbr16, l, rows)
        xs, h = _down_epi(merged, wout16, (l,), xs, rows, seq, mod4, gains4, l, 5, 3, 1.0,
                          nxt=(mod4, gains4, l, 4, 6, 7))
        u = _ffn_up(h, wg16, wu16, l, 1)
        nxt = None if last else (mod4, gains4, l + 1, 0, 0, 1)
        xs, h = _down_epi(u, wd16, (l, 1), xs, rows, seq, mod4, gains4, l, 8, 5, 0.5, nxt=nxt)
    return xs.reshape(batch, seq, d)


def kernel(x, c, ctx, c_ctx, w_ada, b_ada, norm_g, w_ffn_gate, w_ffn_up, w_ffn_down,
           w_in, na_rpb, sw_sink, df_lambda, df_subln_g, w_branch, w_gate, b_gate, w_out):
    return _forward(x, c, ctx, c_ctx, w_ada, b_ada, norm_g, w_ffn_gate, w_ffn_up, w_ffn_down,
                    w_in, na_rpb, sw_sink, df_lambda, df_subln_g, w_branch, w_gate, b_gate, w_out)
```

```python
import functools
import math

import numpy as np
import jax
import jax.numpy as jnp
from jax import lax
from jax.experimental import pallas as pl
from jax.experimental.pallas import tpu as pltpu

F32 = jnp.float32
BF16 = jnp.bfloat16

GRID_W = 64
HEAD_DIM = 128
NA_HEADS = 8
NA_WIN_ROWS = 8
NA_WIN_COLS = 16
SW_HEADS = 8
SW_KV_HEADS = 2
SW_GROUP = SW_HEADS // SW_KV_HEADS
SW_WINDOW = 128
SW_BLOCK = 128
DF_HEADS = 4
DF_Q_BLOCK = 128
N_MOD = 9
ROPE_BASE = 10000.0
EPS = 1e-6
NEG_INF = -1e30
LOG2E = math.log2(math.e)
Q_SCALE = HEAD_DIM ** -0.5 * LOG2E

OFF_QA, OFF_KA, OFF_VA = 0, 1024, 2048
OFF_QB, OFF_KB, OFF_VB = 3072, 4096, 4352
OFF_QC, OFF_KC, OFF_VC = 4608, 5632, 6656
IN_WIDTH = 7680
OFF_OA, OFF_OB, OFF_OC = 0, 1024, 2048
ATT_WIDTH = 3072

V7X_LANES = 128
MOD_ROWS = 8
V7X_VMEM_LIMIT = 58 * 1024 * 1024

TM_UP, TN_UP = 1024, 512
TM_DOWN = 1024
TK_DOWN_CHOICES = (1024, 512)
DOWN_WB_PARTS = 4
TM_PROJ, TN_PROJ = 1024, 512
DF_Q_STEP = 8 * DF_Q_BLOCK
NA_Q_ROWS = 4
NA_KEY_ROWS = 12
NA_UNROLL = 4
SW_UNROLL = 4
TM_MERGE, TN_MERGE = 1024, 256
TN_ADA = 512
EPI_ROWS = 16
EPI_UNROLL = 8


def _params(sem):
    return pltpu.CompilerParams(dimension_semantics=sem, vmem_limit_bytes=V7X_VMEM_LIMIT)


def _dot(a, b):
    return jnp.dot(a, b, preferred_element_type=F32)


def _dot_t(a, b):
    return lax.dot_general(a, b, (((1,), (1,)), ((), ())), preferred_element_type=F32)


def _rstd(y):
    return lax.rsqrt(jnp.mean(y * y, axis=-1, keepdims=True) + EPS)


def _silu(a):
    return a * jax.nn.sigmoid(a)


def _adaln_body(c_ref, w_ref, b_ref, o_ref):
    s = _silu(c_ref[...]).astype(BF16)
    o_ref[...] = _dot(s, w_ref[...].astype(BF16)) + b_ref[...]


def _adaln(cc, w_ada, b_ada):
    depth, d, nd = w_ada.shape
    return pl.pallas_call(
        _adaln_body,
        grid=(depth, nd // TN_ADA),
        in_specs=[
            pl.BlockSpec((MOD_ROWS, d), lambda l, j: (0, 0)),
            pl.BlockSpec((None, d, TN_ADA), lambda l, j: (l, 0, j)),
            pl.BlockSpec((None, 1, TN_ADA), lambda l, j: (l, 0, j)),
        ],
        out_specs=pl.BlockSpec((None, MOD_ROWS, TN_ADA), lambda l, j: (l, 0, j)),
        out_shape=jax.ShapeDtypeStruct((depth, MOD_ROWS, nd), F32),
        compiler_params=_params(("arbitrary", "arbitrary")),
        name="adaln",
    )(cc, w_ada, b_ada.reshape(depth, 1, nd))


def _mod_spec(layer, chunk, d, tm, seq, tile_lag=0):
    return pl.BlockSpec((None, None, 1, d),
                        lambda i, *_: (layer, (jnp.maximum(i - tile_lag, 0) * tm) // seq, 0, chunk))


def _gain_spec(layer, idx, d):
    return pl.BlockSpec((None, None, 1, d), lambda i, *_: (layer, idx, 0, 0))


def _norm_mod_body(x_ref, xtail_ref, g_ref, shift_ref, scale_ref, o_ref, *, head_tiles):
    def emit(src_ref):
        x = src_ref[...]
        y = x * _rstd(x) * g_ref[...]
        o_ref[...] = (y * (1.0 + scale_ref[...]) + shift_ref[...]).astype(BF16)

    i = pl.program_id(0)

    @pl.when(i < head_tiles)
    def _():
        emit(x_ref)

    @pl.when(i >= head_tiles)
    def _():
        emit(xtail_ref)


def _norm_mod(x_head, x_tail, mod4, gains4, layer, seq):
    d = x_head.shape[1]
    tm = 256
    head_tiles = x_head.shape[0] // tm
    tail_tiles = x_tail.shape[0] // tm
    return pl.pallas_call(
        functools.partial(_norm_mod_body, head_tiles=head_tiles),
        grid=(head_tiles + tail_tiles,),
        in_specs=[
            pl.BlockSpec((tm, d), lambda i: (jnp.minimum(i, head_tiles - 1), 0)),
            pl.BlockSpec((tm, d), lambda i: (jnp.maximum(i - head_tiles, 0), 0)),
            _gain_spec(layer, 0, d),
            _mod_spec(layer, 0, d, tm, seq),
            _mod_spec(layer, 1, d, tm, seq),
        ],
        out_specs=pl.BlockSpec((tm, d), lambda i: (i, 0)),
        out_shape=jax.ShapeDtypeStruct(((head_tiles + tail_tiles) * tm, d), BF16),
        compiler_params=_params(("arbitrary",)),
        name="norm_mod",
    )(x_head, x_tail, gains4, mod4, mod4)


def _ffn_up_body(h_ref, wg_ref, wu_ref, o_ref):
    h = h_ref[...]
    a = _dot(h, wg_ref[...])
    b = _dot(h, wu_ref[...])
    o_ref[...] = (_silu(a) * b).astype(BF16)


def _ffn_up(h, w_gate, w_up, layer, which):
    t, d = h.shape
    n_tiles = w_gate.shape[-1] // TN_UP
    wspec = pl.BlockSpec((None, None, d, TN_UP), lambda i, j: (layer, which, 0, j))
    return pl.pallas_call(
        _ffn_up_body,
        grid=(t // TM_UP, n_tiles),
        in_specs=[pl.BlockSpec((TM_UP, d), lambda i, j: (i, 0)), wspec, wspec],
        out_specs=pl.BlockSpec((None, TM_UP, TN_UP), lambda i, j: (j, i, 0)),
        out_shape=jax.ShapeDtypeStruct((n_tiles, t, TN_UP), BF16),
        compiler_params=_params(("arbitrary", "arbitrary")),
        name="ffn_up",
    )(h, w_gate, w_up)


def _down_epi_body(a_ref, w_ref, x_hbm, xtail_hbm, gate_ref, gpost_ref, *rest, coef, nk, tm, with_next,
                   head_tiles):
    if with_next:
        (gnext_ref, shift_ref, scale_ref, pshift_ref, pscale_ref, xo_hbm, ho_hbm,
         acc_ref, stat_ref, xbuf_ref, sem_ref, hbuf_ref) = rest
    else:
        xo_hbm, acc_ref, stat_ref, xbuf_ref, sem_ref = rest
    i = pl.program_id(0)
    k = pl.program_id(1)
    n_tiles = pl.num_programs(0)
    d = acc_ref.shape[-1]
    n_parts = DOWN_WB_PARTS
    part = tm // n_parts
    sem_in, sem_xout, sem_hout = 0, 1, 1 + n_parts

    def tile_rows(tile):
        return pl.ds(pl.multiple_of(tile * tm, tm), tm)

    def residual_copy(src, tile):
        return pltpu.make_async_copy(src.at[tile_rows(tile), :], xbuf_ref, sem_ref.at[sem_in])

    def residual_read(tile, action):
        if head_tiles is None:
            action(residual_copy(x_hbm, tile))
        else:
            @pl.when(tile < head_tiles)
            def _():
                action(residual_copy(x_hbm, tile))

            @pl.when(tile >= head_tiles)
            def _():
                action(residual_copy(xtail_hbm, tile - head_tiles))

    def part_rows(tile, p):
        return pl.ds(pl.multiple_of(tile * tm + p * part, part), part)

    def x_copy(tile, p):
        return pltpu.make_async_copy(xbuf_ref.at[pl.ds(p * part, part), :], xo_hbm.at[part_rows(tile, p), :],
                                     sem_ref.at[sem_xout + p])

    def h_copy(tile, p):
        return pltpu.make_async_copy(hbuf_ref, ho_hbm.at[part_rows(tile, p), :], sem_ref.at[sem_hout])

    def partial_product():
        if len(a_ref.shape) == 3:
            a = jnp.concatenate([a_ref[j] for j in range(a_ref.shape[0])], axis=1)
        else:
            a = a_ref[...]
        return _dot(a, w_ref[...])

    lanes = stat_ref.shape[-1]
    n_chunks = tm // EPI_ROWS
    part_chunks = n_chunks // n_parts

    def chunk_rows(c):
        return pl.ds(pl.multiple_of(c * EPI_ROWS, EPI_ROWS), EPI_ROWS)

    def put_stat(slot, rows, val):
        stat_ref[slot, rows, :] = jnp.broadcast_to(val, (EPI_ROWS, lanes))

    def get_stat(slot, rows):
        return jnp.tile(stat_ref[slot, rows, :], (1, d // lanes))

    def modulate_part(p, shift_r, scale_r):
        next_gain = gnext_ref[...] * (1.0 + scale_r[...])
        shift = shift_r[...]
        for c in range(part_chunks):
            rows = chunk_rows(p * part_chunks + c)
            hn = (xbuf_ref[rows, :] * get_stat(1, rows)) * next_gain + shift
            hbuf_ref[c * EPI_ROWS:(c + 1) * EPI_ROWS, :] = hn.astype(BF16)

    finishing = jnp.logical_and(jnp.logical_and(k >= 1, k <= n_parts), i > 0)

    @pl.when(k == 0)
    def _():
        acc_ref[...] = partial_product()

    @pl.when(finishing)
    def _():
        acc_ref[...] += partial_product()
        if with_next:
            @pl.when(k > 1)
            def _():
                h_copy(i - 1, k - 2).wait()

            modulate_part(k - 1, pshift_ref, pscale_ref)

    @pl.when(jnp.logical_and(jnp.logical_and(k != 0, k != nk - 1), jnp.logical_not(finishing)))
    def _():
        acc_ref[...] += partial_product()

    @pl.when(k == nk - 1)
    def _():
        y = acc_ref[...] + partial_product()
        acc_ref[...] = y
        stat_ref[0] = jnp.broadcast_to(_rstd(y), (tm, lanes))

    for p in range(n_parts):
        @pl.when(jnp.logical_and(k == p + 1, i > 0))
        def _(p=p):
            x_copy(i - 1, p).start()
            if with_next:
                h_copy(i - 1, p).start()

    @pl.when(k == n_parts + 1)
    def _():
        @pl.when(i > 0)
        def _():
            for p in range(n_parts):
                x_copy(i - 1, p).wait()
            if with_next:
                h_copy(i - 1, n_parts - 1).wait()

        residual_read(i, lambda copy: copy.start())

    @pl.when(k == nk - 1)
    def _():
        gate_gain = (coef * gate_ref[...]) * gpost_ref[...]

        def pass_residual(c, carry):
            rows = chunk_rows(c)
            xn = xbuf_ref[rows, :] + (acc_ref[rows, :] * get_stat(0, rows)) * gate_gain
            xbuf_ref[rows, :] = xn
            if with_next:
                put_stat(1, rows, _rstd(xn))
            return carry

        residual_read(i, lambda copy: copy.wait())
        lax.fori_loop(0, n_chunks, pass_residual, 0, unroll=EPI_UNROLL)

        @pl.when(i == n_tiles - 1)
        def _():
            for p in range(n_parts):
                x_copy(i, p).start()
            if with_next:
                def finish_part(p, carry):
                    modulate_part(p, shift_ref, scale_ref)
                    h_copy(i, p).start()
                    h_copy(i, p).wait()
                    return carry

                lax.fori_loop(0, n_parts, finish_part, 0)
            for p in range(n_parts):
                x_copy(i, p).wait()


def _down_epi(a, w, wsel, xs, rows, seq, mod4, gains4, layer, gate_chunk, gpost_idx, coef,
              nxt=None):
    tm = TM_DOWN
    kdim = a.shape[0] * a.shape[2] if a.ndim == 3 else a.shape[1]
    tk = next(c for c in TK_DOWN_CHOICES if kdim % c == 0 and kdim // c - 1 > DOWN_WB_PARTS + 1)
    if isinstance(xs, tuple):
        x_head, x_tail = xs
        assert x_head.shape[0] % tm == 0 and x_tail.shape[0] % tm == 0
        head_tiles = x_head.shape[0] // tm
    else:
        x_head, x_tail, head_tiles = xs, xs, None
    d = x_head.shape[1]
    if a.ndim == 3:
        per_step = tk // a.shape[2]
        assert per_step * a.shape[2] == tk and a.shape[0] % per_step == 0
        nk = a.shape[0] // per_step
        aspec = pl.BlockSpec((per_step, tm, a.shape[2]), lambda i, k: (k, i, 0))
    else:
        nk = a.shape[1] // tk
        aspec = pl.BlockSpec((tm, tk), lambda i, k: (i, k))
    assert nk - 1 > DOWN_WB_PARTS + 1 and rows % tm == 0
    assert tm % (DOWN_WB_PARTS * EPI_ROWS * EPI_UNROLL) == 0
    nlead = len(wsel)
    wspec = pl.BlockSpec((None,) * nlead + (tk, d), lambda i, k: tuple(wsel) + (k, 0))
    hbm = pl.BlockSpec(memory_space=pl.ANY)
    in_specs = [
        aspec,
        wspec,
        hbm,
        hbm,
        _mod_spec(layer, gate_chunk, d, tm, seq),
        _gain_spec(layer, gpost_idx, d),
    ]
    args = [a, w, x_head, x_tail, mod4, gains4]
    out_specs = [hbm]
    out_shape = [jax.ShapeDtypeStruct((rows, d), F32)]
    scratch = [
        pltpu.VMEM((tm, d), F32),
        pltpu.VMEM((2, tm, V7X_LANES), F32),
        pltpu.VMEM((tm, d), F32),
        pltpu.SemaphoreType.DMA((2 + DOWN_WB_PARTS,)),
    ]
    if nxt is not None:
        mod_n, gains_n, layer_n, gain_idx, shift_chunk, scale_chunk = nxt
        in_specs += [
            _gain_spec(layer_n, gain_idx, d),
            _mod_spec(layer_n, shift_chunk, d, tm, seq),
            _mod_spec(layer_n, scale_chunk, d, tm, seq),
            _mod_spec(layer_n, shift_chunk, d, tm, seq, tile_lag=1),
            _mod_spec(layer_n, scale_chunk, d, tm, seq, tile_lag=1),
        ]
        args += [gains_n, mod_n, mod_n, mod_n, mod_n]
        out_specs.append(hbm)
        out_shape.append(jax.ShapeDtypeStruct((rows, d), BF16))
        scratch.append(pltpu.VMEM((tm // DOWN_WB_PARTS, d), BF16))
    body = functools.partial(_down_epi_body, coef=coef, nk=nk, tm=tm, with_next=nxt is not None,
                             head_tiles=head_tiles)
    out = pl.pallas_call(
        body,
        grid=(rows // tm, nk),
        in_specs=in_specs,
        out_specs=out_specs,
        out_shape=out_shape,
        scratch_shapes=scratch,
        compiler_params=_params(("arbitrary", "arbitrary")),
        name="down_epi",
    )(*args)
    return (out[0], out[1]) if nxt is not None else (out[0], None)


def _in_tiles(j, lo, hi):
    return jnp.logical_and(j >= lo // TN_PROJ, j < hi // TN_PROJ)


def _proj_body(h_ref, w_ref, cos_ref, sin_ref, o_ref, *, n_lat_tiles):
    i = pl.program_id(0)
    j = pl.program_id(1)
    hd = HEAD_DIM
    acc = _dot(h_ref[...], w_ref[...].astype(BF16))
    is_q =_in_tiles(j, OFF_QA, OFF_KA) | _in_tiles(j, OFF_QB, OFF_KB) | _in_tiles(j, OFF_QC, OFF_KC)
    qscale = jnp.where(is_q, Q_SCALE, 1.0).astype(F32)
    latent = i < n_lat_tiles
    rope_full = latent & (_in_tiles(j, OFF_QB, OFF_KB) | _in_tiles(j, OFF_QC, OFF_VC))
    rope_half = latent & _in_tiles(j, OFF_KB, OFF_QC)

    def rope_heads(n_heads, scale=None):
        cos = cos_ref[...]
        sin = sin_ref[...]
        if scale is not None:
            cos = cos * scale
            sin = sin * scale
        lane = lax.broadcasted_iota(jnp.int32, cos.shape, 1)
        quarter = hd // 4
        first = (lane % (2 * quarter)) < quarter
        for hh in range(n_heads):
            a = acc[:, hh * hd:(hh + 1) * hd]
            partner = jnp.where(first, pltpu.roll(a, hd - quarter, 1), pltpu.roll(a, quarter, 1))
            o_ref[:, hh * hd:(hh + 1) * hd] = (a * cos + partner * sin).astype(BF16)

    @pl.when(rope_full)
    def _():
        rope_heads(TN_PROJ // hd, qscale)

    @pl.when(rope_half)
    def _():
        n_rope = (OFF_VB - OFF_KB) // hd
        rope_heads(n_rope)
        o_ref[:, n_rope * hd:] = acc[:, n_rope * hd:].astype(BF16)

    @pl.when(jnp.logical_not(rope_full | rope_half))
    def _():
        o_ref[...] = (acc * qscale).astype(BF16)


def _proj(h, w_in, layer, cos_t, sin_t, seq, t_lat):
    t, d = h.shape
    width = w_in.shape[-1]
    n_tiles = width // TN_PROJ
    pos_tiles = seq // TM_PROJ
    assert OFF_KB % TN_PROJ == 0 and OFF_QC - OFF_KB == TN_PROJ
    tspec = pl.BlockSpec((TM_PROJ, HEAD_DIM), lambda i, j: (i % pos_tiles, 0))
    return pl.pallas_call(
        functools.partial(_proj_body, n_lat_tiles=t_lat // TM_PROJ),
        grid=(t // TM_PROJ, n_tiles),
        in_specs=[
            pl.BlockSpec((TM_PROJ, d), lambda i, j: (i, 0)),
            pl.BlockSpec((None, d, TN_PROJ), lambda i, j: (layer, 0, j)),
            tspec, tspec,
        ],
        out_specs=pl.BlockSpec((TM_PROJ, TN_PROJ), lambda i, j: (i, j)),
        out_shape=jax.ShapeDtypeStruct((t, width), BF16),
        compiler_params=_params(("arbitrary", "arbitrary")),
        name="proj_rope",
    )(h, w_in, cos_t, sin_t)


def _rope_tables(n):
    t = jnp.arange(n, dtype=jnp.int32)
    row = (t // GRID_W).astype(F32)
    col = (t % GRID_W).astype(F32)
    half = HEAD_DIM // 2
    inv = ROPE_BASE ** (-jnp.arange(0, half, 2, dtype=F32) / half)
    ang_r = row[:, None] * inv
    ang_c = col[:, None] * inv
    cos_h = jnp.concatenate([jnp.cos(ang_r)] * 2 + [jnp.cos(ang_c)] * 2, axis=-1)
    sin_h = jnp.concatenate([-jnp.sin(ang_r), jnp.sin(ang_r), -jnp.sin(ang_c), jnp.sin(ang_c)], axis=-1)
    return cos_h, sin_h


def _na_key_row_start(blk, rows):
    return np.clip(blk * NA_Q_ROWS - NA_WIN_ROWS // 2, 0, rows - NA_KEY_ROWS)


def _na_bias_table(rpb, rows):
    kh, kw = NA_WIN_ROWS, NA_WIN_COLS
    nblk = rows // NA_Q_ROWS
    assert rows % NA_Q_ROWS == 0 and nblk >= 3 and rows >= NA_KEY_ROWS
    qc = np.arange(GRID_W)
    kc = np.arange(GRID_W)
    col_start = np.clip(qc - kw // 2, 0, GRID_W - kw)
    col_ok = (kc[None, :] >= col_start[:, None]) & (kc[None, :] < col_start[:, None] + kw)
    dc = kc[None, :] - qc[:, None] + NA_WIN_COLS - 1
    col_hot = (dc[None] == np.arange(2 * NA_WIN_COLS - 1)[:, None, None]).astype(np.float32)
    n_dr = 2 * NA_WIN_ROWS - 1
    row_hot = np.zeros((3, NA_Q_ROWS, NA_KEY_ROWS, n_dr), np.float32)
    for cls, blk in enumerate((0, 1, nblk - 1)):
        ks = _na_key_row_start(blk, rows)
        for j in range(NA_Q_ROWS):
            r = blk * NA_Q_ROWS + j
            rs = np.clip(r - kh // 2, 0, rows - kh)
            for i in range(NA_KEY_ROWS):
                if rs <= ks + i < rs + kh:
                    row_hot[cls, j, i, ks + i - r + NA_WIN_ROWS - 1] = 1.0
    b = jnp.einsum('cjir,hrd,dqk->hcjqik', row_hot, rpb.astype(F32) * LOG2E, col_hot,
                   precision=lax.Precision.HIGHEST)
    ok = (row_hot.sum(-1) > 0)[:, :, None, :, None] & col_ok[None, None, :, None, :]
    b = jnp.where(ok[None], b, NEG_INF)
    return b.reshape(rpb.shape[0], 3, NA_Q_ROWS * GRID_W, NA_KEY_ROWS * GRID_W)


def _softmax_parts(parts, extra=None):
    m = functools.reduce(jnp.maximum, [jnp.max(p, axis=-1, keepdims=True) for p in parts])
    if extra is not None:
        m = jnp.maximum(m, extra)
    es = [jnp.exp2(p - m) for p in parts]
    l = functools.reduce(jnp.add, [jnp.sum(e, axis=-1, keepdims=True) for e in es])
    if extra is not None:
        l = l + jnp.exp2(extra - m)
    inv = 1.0 / l
    return [e * inv for e in es]


def _na_body(q_ref, k_ref, v_ref, kx_ref, vx_ref, bias_ref, att_in_ref, o_ref, *, rows):
    del att_in_ref
    nblk = rows // NA_Q_ROWS
    nq = NA_Q_ROWS * GRID_W
    nkeys = NA_KEY_ROWS * GRID_W
    kx = kx_ref[...]
    vx = vx_ref[...]

    def blk_fn(bi, carry):
        ks = jnp.clip(bi * NA_Q_ROWS - NA_WIN_ROWS // 2, 0, rows - NA_KEY_ROWS)
        cls = jnp.where(bi == 0, 0, jnp.where(bi == nblk - 1, 2, 1))
        qrows = pl.ds(pl.multiple_of(bi * nq, nq), nq)
        krows = pl.ds(pl.multiple_of(ks * GRID_W, GRID_W), nkeys)
        q = q_ref[qrows, :]
        s_loc = _dot_t(q, k_ref[krows, :]) + bias_ref[cls]
        s_ctx = _dot_t(q, kx)
        p_loc, p_ctx = _softmax_parts([s_loc, s_ctx])
        o = _dot(p_loc.astype(BF16), v_ref[krows, :]) + _dot(p_ctx.astype(BF16), vx)
        o_ref[qrows, :] = o.astype(BF16)
        return carry

    lax.fori_loop(0, nblk, blk_fn, 0, unroll=NA_UNROLL)


def _na(qkv, att, bias_tbl, batch, seq, ctx_len):
    rows = seq // GRID_W
    hd = HEAD_DIM
    ctx_blk0 = batch * seq // ctx_len

    def col(off):
        return off // hd

    return pl.pallas_call(
        functools.partial(_na_body, rows=rows),
        grid=(batch, NA_HEADS),
        in_specs=[
            pl.BlockSpec((seq, hd), lambda b, h: (b, col(OFF_QA) + h)),
            pl.BlockSpec((seq, hd), lambda b, h: (b, col(OFF_KA) + h)),
            pl.BlockSpec((seq, hd), lambda b, h: (b, col(OFF_VA) + h)),
            pl.BlockSpec((ctx_len, hd), lambda b, h: (ctx_blk0 + b, col(OFF_KA) + h)),
            pl.BlockSpec((ctx_len, hd), lambda b, h: (ctx_blk0 + b, col(OFF_VA) + h)),
            pl.BlockSpec((None,) + bias_tbl.shape[1:], lambda b, h: (h, 0, 0, 0)),
            pl.BlockSpec(memory_space=pl.ANY),
        ],
        out_specs=pl.BlockSpec((seq, hd), lambda b, h: (b, col(OFF_OA) + h)),
        out_shape=jax.ShapeDtypeStruct(att.shape, att.dtype),
        input_output_aliases={6: 0},
        compiler_params=_params(("arbitrary", "arbitrary")),
        name="na_attn",
    )(qkv, qkv, qkv, qkv, qkv, bias_tbl, att)


def _sw_body(sink_ref, q_ref, k_ref, v_ref, kx_ref, vx_ref, att_in_ref, o_ref, *, seq):
    del att_in_ref
    kv = pl.program_id(1)
    blk = SW_BLOCK
    span = 3 * blk
    g = SW_GROUP
    kx = kx_ref[...]
    vx = vx_ref[...]
    grp = lax.broadcasted_iota(jnp.int32, (g * blk, 1), 0) // blk
    sink_col = jnp.zeros((g * blk, 1), F32)
    for gi in range(g):
        sink_col = jnp.where(grp == gi, sink_ref[kv * g + gi] * LOG2E, sink_col)
    qoff = lax.broadcasted_iota(jnp.int32, (g * blk, span), 0) % blk
    koff = lax.broadcasted_iota(jnp.int32, (g * blk, span), 1)

    def blk_fn(n, carry):
        start = jnp.clip((n - 1) * blk, 0, seq - span)
        qrows = pl.ds(pl.multiple_of(n * blk, blk), blk)
        krows = pl.ds(pl.multiple_of(start, blk), span)
        qb = q_ref[qrows, :]
        qs = jnp.concatenate([qb[:, gi * HEAD_DIM:(gi + 1) * HEAD_DIM] for gi in range(g)], axis=0)
        s_loc = _dot_t(qs, k_ref[krows, :])
        dist = (n * blk + qoff) - (start + koff)
        s_loc = jnp.where(jnp.abs(dist) <= SW_WINDOW, s_loc, NEG_INF)
        s_ctx = _dot_t(qs, kx)
        p_loc, p_ctx = _softmax_parts([s_loc, s_ctx], extra=sink_col)
        o = _dot(p_loc.astype(BF16), v_ref[krows, :]) + _dot(p_ctx.astype(BF16), vx)
        for gi in range(g):
            o_ref[qrows, gi * HEAD_DIM:(gi + 1) * HEAD_DIM] = o[gi * blk:(gi + 1) * blk].astype(BF16)
        return carry

    lax.fori_loop(0, seq // blk, blk_fn, 0, unroll=SW_UNROLL)


def _sw(qkv, att, sink, batch, seq, ctx_len):
    hd = HEAD_DIM
    gw = SW_GROUP * hd
    ctx_blk0 = batch * seq // ctx_len
    return pl.pallas_call(
        functools.partial(_sw_body, seq=seq),
        grid=(batch, SW_KV_HEADS),
        in_specs=[
            pl.BlockSpec(memory_space=pltpu.SMEM),
            pl.BlockSpec((seq, gw), lambda b, k: (b, OFF_QB // gw + k)),
            pl.BlockSpec((seq, hd), lambda b, k: (b, OFF_KB // hd + k)),
            pl.BlockSpec((seq, hd), lambda b, k: (b, OFF_VB // hd + k)),
            pl.BlockSpec((ctx_len, hd), lambda b, k: (ctx_blk0 + b, OFF_KB // hd + k)),
            pl.BlockSpec((ctx_len, hd), lambda b, k: (ctx_blk0 + b, OFF_VB // hd + k)),
            pl.BlockSpec(memory_space=pl.ANY),
        ],
        out_specs=pl.BlockSpec((seq, gw), lambda b, k: (b, OFF_OB // gw + k)),
        out_shape=jax.ShapeDtypeStruct(att.shape, att.dtype),
        input_output_aliases={6: 0},
        compiler_params=_params(("arbitrary", "arbitrary")),
        name="sw_attn",
    )(sink, qkv, qkv, qkv, qkv, qkv, att)


def _df_lambda(lam_ref, lambda_init):
    lv = lam_ref[...]
    s01 = jnp.sum(lv[0:1] * lv[1:2], axis=-1, keepdims=True)
    s23 = jnp.sum(lv[2:3] * lv[3:4], axis=-1, keepdims=True)
    return jnp.exp(s01) - jnp.exp(s23) + lambda_init


def _df_core(q, k_parts, v_parts, lam, subln, lambda_init):
    hd = HEAD_DIM
    exps, weights = [], []
    for c in range(2):
        qc = q[:, c * hd:(c + 1) * hd]
        scores = [_dot_t(qc, kp[:, c * hd:(c + 1) * hd]) for kp in k_parts]
        m = functools.reduce(jnp.maximum, [jnp.max(s, axis=-1, keepdims=True) for s in scores])
        es = [jnp.exp2(s - m) for s in scores]
        l = functools.reduce(jnp.add, [jnp.sum(e, axis=-1, keepdims=True) for e in es])
        exps.append(es)
        weights.append(1.0 / l)
    w0 = weights[0]
    w1 = lam * weights[1]
    o = None
    for i, vp in enumerate(v_parts):
        a = (exps[0][i] * w0 - exps[1][i] * w1).astype(BF16)
        term = _dot(a, vp[...])
        o = term if o is None else o + term
    return o * _rstd(o) * subln * (1.0 - lambda_init)


def _df_body(lam_ref, subln_ref, q_ref, k_ref, v_ref, kx_ref, vx_ref, att_in_ref, o_ref, *,
             lambda_init):
    del att_in_ref
    lam = _df_lambda(lam_ref, lambda_init)
    subln = subln_ref[...]
    for sub in range(DF_Q_STEP // DF_Q_BLOCK):
        rows = slice(sub * DF_Q_BLOCK, (sub + 1) * DF_Q_BLOCK)
        o = _df_core(q_ref[rows, :], [k_ref, kx_ref], [v_ref, vx_ref], lam, subln, lambda_init)
        o_ref[rows, :] = o.astype(BF16)


def _df(qkv, att, lam4, subln4, layer, lambda_init, batch, seq, ctx_len):
    w = 2 * HEAD_DIM
    nqb = seq // DF_Q_STEP
    ctx_blk0 = batch * seq // ctx_len
    return pl.pallas_call(
        functools.partial(_df_body, lambda_init=lambda_init),
        grid=(batch, DF_HEADS, nqb),
        in_specs=[
            pl.BlockSpec((None, 4, HEAD_DIM), lambda b, h, n: (layer, 0, 0)),
            pl.BlockSpec((None, 1, w), lambda b, h, n: (layer, 0, 0)),
            pl.BlockSpec((DF_Q_STEP, w), lambda b, h, n: (b * nqb + n, OFF_QC // w + h)),
            pl.BlockSpec((seq, w), lambda b, h, n: (b, OFF_KC // w + h)),
            pl.BlockSpec((seq, w), lambda b, h, n: (b, OFF_VC // w + h)),
            pl.BlockSpec((ctx_len, w), lambda b, h, n: (ctx_blk0 + b, OFF_KC // w + h)),
            pl.BlockSpec((ctx_len, w), lambda b, h, n: (ctx_blk0 + b, OFF_VC // w + h)),
            pl.BlockSpec(memory_space=pl.ANY),
        ],
        out_specs=pl.BlockSpec((DF_Q_STEP, w), lambda b, h, n: (b * nqb + n, OFF_OC // w + h)),
        out_shape=jax.ShapeDtypeStruct(att.shape, att.dtype),
        input_output_aliases={7: 0},
        compiler_params=_params(("arbitrary", "arbitrary", "arbitrary")),
        name="df_attn",
    )(lam4, subln4, qkv, qkv, qkv, qkv, qkv, att)


def _ctx_body(sink_ref, lam_ref, subln_ref, x_ref, att_in_ref, o_ref, *, lambda_init):
    del att_in_ref
    hd = HEAD_DIM

    def cols(off, width=hd):
        return x_ref[:, off:off + width]

    for h in range(NA_HEADS):
        s = _dot_t(cols(OFF_QA + h * hd), cols(OFF_KA + h * hd))
        (p,) = _softmax_parts([s])
        o = _dot(p.astype(BF16), cols(OFF_VA + h * hd))
        o_ref[:, OFF_OA + h * hd:OFF_OA + (h + 1) * hd] = o.astype(BF16)

    for hq in range(SW_HEADS):
        kvh = hq // SW_GROUP
        s = _dot_t(cols(OFF_QB + hq * hd), cols(OFF_KB + kvh * hd))
        sink = jnp.full((s.shape[0], 1), sink_ref[hq] * LOG2E, F32)
        (p,) = _softmax_parts([s], extra=sink)
        o = _dot(p.astype(BF16), cols(OFF_VB + kvh * hd))
        o_ref[:, OFF_OB + hq * hd:OFF_OB + (hq + 1) * hd] = o.astype(BF16)

    lam = _df_lambda(lam_ref, lambda_init)
    subln = subln_ref[...]
    for h in range(DF_HEADS):
        w = 2 * hd
        o = _df_core(cols(OFF_QC + h * w, w), [cols(OFF_KC + h * w, w)], [cols(OFF_VC + h * w, w)],
                     lam, subln, lambda_init)
        o_ref[:, OFF_OC + h * w:OFF_OC + (h + 1) * w] = o.astype(BF16)


def _ctx_attn(qkv, att, sink, lam4, subln4, layer, lambda_init, batch, seq, ctx_len):
    ctx_blk0 = batch * seq // ctx_len
    return pl.pallas_call(
        functools.partial(_ctx_body, lambda_init=lambda_init),
        grid=(batch,),
        in_specs=[
            pl.BlockSpec(memory_space=pltpu.SMEM),
            pl.BlockSpec((None, 4, HEAD_DIM), lambda b: (layer, 0, 0)),
            pl.BlockSpec((None, 1, 2 * HEAD_DIM), lambda b: (layer, 0, 0)),
            pl.BlockSpec((ctx_len, IN_WIDTH), lambda b: (ctx_blk0 + b, 0)),
            pl.BlockSpec(memory_space=pl.ANY),
        ],
        out_specs=pl.BlockSpec((ctx_len, ATT_WIDTH), lambda b: (ctx_blk0 + b, 0)),
        out_shape=jax.ShapeDtypeStruct(att.shape, att.dtype),
        input_output_aliases={4: 0},
        compiler_params=_params(("arbitrary",)),
        name="ctx_attn",
    )(sink, lam4, subln4, qkv, att)


def _merge_body(h_ref, att_ref, wg_ref, bg_ref, wb_ref, o_ref, *, branch_width):
    h = h_ref[...]
    acc = None
    for i in range(3):
        gate = jax.nn.sigmoid(_dot(h, wg_ref[i]) + bg_ref[i])
        br = _dot(att_ref[:, i * branch_width:(i + 1) * branch_width], wb_ref[i])
        acc = gate * br if acc is None else acc + gate * br
    o_ref[...] = acc.astype(BF16)


def _merge(h, att, w_gate, b_gate4, w_branch, layer, rows):
    d = h.shape[1]
    nb, bw = w_branch.shape[1], w_branch.shape[2]
    tm, tn = TM_MERGE, TN_MERGE
    return pl.pallas_call(
        functools.partial(_merge_body, branch_width=bw),
        grid=(rows // tm, d // tn),
        in_specs=[
            pl.BlockSpec((tm, d), lambda i, j: (i, 0)),
            pl.BlockSpec((tm, nb * bw), lambda i, j: (i, 0)),
            pl.BlockSpec((None, nb, d, tn), lambda i, j: (layer, 0, 0, j)),
            pl.BlockSpec((None, nb, 1, tn), lambda i, j: (layer, 0, 0, j)),
            pl.BlockSpec((None, nb, bw, tn), lambda i, j: (layer, 0, 0, j)),
        ],
        out_specs=pl.BlockSpec((tm, tn), lambda i, j: (i, j)),
        out_shape=jax.ShapeDtypeStruct((rows, d), BF16),
        compiler_params=_params(("arbitrary", "arbitrary")),
        name="merge",
    )(h, att, w_gate, b_gate4, w_branch)


@jax.jit
def _forward(x, c, ctx, c_ctx, w_ada, b_ada, norm_g, w_ffn_gate, w_ffn_up, w_ffn_down,
             w_in, na_rpb, sw_sink, df_lambda, df_subln_g, w_branch, w_gate, b_gate, w_out):
    batch, seq, d = x.shape
    ctx_len = ctx.shape[1]
    depth = w_ada.shape[0]
    t_lat = batch * seq
    t = t_lat + batch * ctx_len
    assert seq % TM_UP == 0 and (batch * ctx_len) % TM_UP == 0 and batch + 1 <= MOD_ROWS
    assert seq % ctx_len == 0 and ctx_len % SW_BLOCK == 0

    xs = (x.reshape(t_lat, d), ctx.reshape(batch * ctx_len, d))
    cc = jnp.zeros((MOD_ROWS, d), F32).at[:batch].set(c).at[batch].set(c_ctx)
    mod4 = _adaln(cc, w_ada, b_ada).reshape(depth, MOD_ROWS, 1, N_MOD * d)
    gains4 = norm_g.reshape(depth, norm_g.shape[1], 1, d)

    wg16 = w_ffn_gate.astype(BF16)
    wu16 = w_ffn_up.astype(BF16)
    wd16 = w_ffn_down.astype(BF16)
    wbr16 = w_branch.astype(BF16)
    wgt16 = w_gate.astype(BF16)
    wout16 = w_out.astype(BF16)
    b_gate4 = b_gate.reshape(depth, b_gate.shape[1], 1, d)
    lam4 = df_lambda.astype(F32)
    subln4 = df_subln_g.reshape(depth, 1, -1)
    cos_t, sin_t = _rope_tables(seq)

    h = _norm_mod(xs[0], xs[1], mod4, gains4, 0, seq)
    for l in range(depth):
        last = l == depth - 1
        lambda_init = 0.8 - 0.6 * math.exp(-0.3 * l)
        u = _ffn_up(h, wg16, wu16, l, 0)
        xs, h = _down_epi(u, wd16, (l, 0), xs, t, seq, mod4, gains4, l, 2, 1, 0.5,
                          nxt=(mod4, gains4, l, 2, 3, 4))
        qkv = _proj(h, w_in, l, cos_t, sin_t, seq, t_lat)
        rows = t_lat if last else t
        att = jnp.zeros((rows, ATT_WIDTH), BF16)
        att = _na(qkv, att, _na_bias_table(na_rpb[l], seq // GRID_W), batch, seq, ctx_len)
        att = _sw(qkv, att, sw_sink[l], batch, seq, ctx_len)
        att = _df(qkv, att, lam4, subln4, l, lambda_init, batch, seq, ctx_len)
        if not last:
            att = _ctx_attn(qkv, att, sw_sink[l], lam4, subln4, l, lambda_init, batch, seq, ctx_len)
        merged = _merge(h, att, wgt16, b_gate4, wbr16, l, rows)
        xs, h = _down_epi(merged, wout16, (l,), xs, rows, seq, mod4, gains4, l, 5, 3, 1.0,
                          nxt=(mod4, gains4, l, 4, 6, 7))
        u = _ffn_up(h, wg16, wu16, l, 1)
        nxt = None if last else (mod4, gains4, l + 1, 0, 0, 1)
        xs, h = _down_epi(u, wd16, (l, 1), xs, rows, seq, mod4, gains4, l, 8, 5, 0.5, nxt=nxt)
    return xs.reshape(batch, seq, d)


def kernel(x, c, ctx, c_ctx, w_ada, b_ada, norm_g, w_ffn_gate, w_ffn_up, w_ffn_down,
           w_in, na_rpb, sw_sink, df_lambda, df_subln_g, w_branch, w_gate, b_gate, w_out):
    return _forward(x, c, ctx, c_ctx, w_ada, b_ada, norm_g, w_ffn_gate, w_ffn_up, w_ffn_down,
                    w_in, na_rpb, sw_sink, df_lambda, df_subln_g, w_branch, w_gate, b_gate, w_out)
```

```python
import functools
import math

import numpy as np
import jax
import jax.numpy as jnp
from jax import lax
from jax.experimental import pallas as pl
from jax.experimental.pallas import tpu as pltpu

F32 = jnp.float32
BF16 = jnp.bfloat16

GRID_W = 64
HEAD_DIM = 128
NA_HEADS = 8
NA_WIN_ROWS = 8
NA_WIN_COLS = 16
SW_HEADS = 8
SW_KV_HEADS = 2
SW_GROUP = SW_HEADS // SW_KV_HEADS
SW_WINDOW = 128
SW_BLOCK = 128
DF_HEADS = 4
DF_Q_BLOCK = 128
N_MOD = 9
ROPE_BASE = 10000.0
EPS = 1e-6
NEG_INF = -1e30
LOG2E = math.log2(math.e)
Q_SCALE = HEAD_DIM ** -0.5 * LOG2E

OFF_QA, OFF_KA, OFF_VA = 0, 1024, 2048
OFF_QB, OFF_KB, OFF_VB = 3072, 4096, 4352
OFF_QC, OFF_KC, OFF_VC = 4608, 5632, 6656
IN_WIDTH = 7680
OFF_OA, OFF_OB, OFF_OC = 0, 1024, 2048
ATT_WIDTH = 3072

V7X_LANES = 128
MOD_ROWS = 8
V7X_VMEM_LIMIT = 58 * 1024 * 1024

TM_UP, TN_UP = 1024, 512
TM_DOWN = 1024
TK_DOWN_CHOICES = (1024, 512)
DOWN_WB_PARTS = 4
TM_PROJ, TN_PROJ = 1024, 512
DF_Q_STEP = 8 * DF_Q_BLOCK
NA_Q_ROWS = 4
NA_KEY_ROWS = 12
NA_UNROLL = 4
SW_UNROLL = 4
TM_MERGE, TN_MERGE = 1024, 256
TN_ADA = 1024
TM_NORM = 512
EPI_ROWS = 16
EPI_UNROLL = 8


def _params(sem):
    return pltpu.CompilerParams(dimension_semantics=sem, vmem_limit_bytes=V7X_VMEM_LIMIT)


def _dot(a, b):
    return jnp.dot(a, b, preferred_element_type=F32)


def _dot_t(a, b):
    return lax.dot_general(a, b, (((1,), (1,)), ((), ())), preferred_element_type=F32)


def _rstd(y):
    return lax.rsqrt(jnp.mean(y * y, axis=-1, keepdims=True) + EPS)


def _silu(a):
    return a * jax.nn.sigmoid(a)


def _adaln_body(c_ref, w_ref, b_ref, o_ref):
    s = _silu(c_ref[...]).astype(BF16)
    o_ref[...] = _dot(s, w_ref[...].astype(BF16)) + b_ref[...]


def _adaln(cc, w_ada, b_ada):
    depth, d, nd = w_ada.shape
    return pl.pallas_call(
        _adaln_body,
        grid=(depth, nd // TN_ADA),
        in_specs=[
            pl.BlockSpec((MOD_ROWS, d), lambda l, j: (0, 0)),
            pl.BlockSpec((None, d, TN_ADA), lambda l, j: (l, 0, j)),
            pl.BlockSpec((None, 1, TN_ADA), lambda l, j: (l, 0, j)),
        ],
        out_specs=pl.BlockSpec((None, MOD_ROWS, TN_ADA), lambda l, j: (l, 0, j)),
        out_shape=jax.ShapeDtypeStruct((depth, MOD_ROWS, nd), F32),
        compiler_params=_params(("arbitrary", "arbitrary")),
        name="adaln",
    )(cc, w_ada, b_ada.reshape(depth, 1, nd))


def _mod_spec(layer, chunk, d, tm, seq, tile_lag=0):
    return pl.BlockSpec((None, None, 1, d),
                        lambda i, *_: (layer, (jnp.maximum(i - tile_lag, 0) * tm) // seq, 0, chunk))


def _gain_spec(layer, idx, d):
    return pl.BlockSpec((None, None, 1, d), lambda i, *_: (layer, idx, 0, 0))


def _norm_mod_body(x_ref, xtail_ref, g_ref, shift_ref, scale_ref, o_ref, *, head_tiles):
    def emit(src_ref):
        x = src_ref[...]
        y = x * _rstd(x) * g_ref[...]
        o_ref[...] = (y * (1.0 + scale_ref[...]) + shift_ref[...]).astype(BF16)

    i = pl.program_id(0)

    @pl.when(i < head_tiles)
    def _():
        emit(x_ref)

    @pl.when(i >= head_tiles)
    def _():
        emit(xtail_ref)


def _norm_mod(x_head, x_tail, mod4, gains4, layer, seq):
    d = x_head.shape[1]
    tm = TM_NORM
    assert x_head.shape[0] % tm == 0 and x_tail.shape[0] % tm == 0
    head_tiles = x_head.shape[0] // tm
    tail_tiles = x_tail.shape[0] // tm
    return pl.pallas_call(
        functools.partial(_norm_mod_body, head_tiles=head_tiles),
        grid=(head_tiles + tail_tiles,),
        in_specs=[
            pl.BlockSpec((tm, d), lambda i: (jnp.minimum(i, head_tiles - 1), 0)),
            pl.BlockSpec((tm, d), lambda i: (jnp.maximum(i - head_tiles, 0), 0)),
            _gain_spec(layer, 0, d),
            _mod_spec(layer, 0, d, tm, seq),
            _mod_spec(layer, 1, d, tm, seq),
        ],
        out_specs=pl.BlockSpec((tm, d), lambda i: (i, 0)),
        out_shape=jax.ShapeDtypeStruct(((head_tiles + tail_tiles) * tm, d), BF16),
        compiler_params=_params(("arbitrary",)),
        name="norm_mod",
    )(x_head, x_tail, gains4, mod4, mod4)


def _ffn_up_body(h_ref, wg_ref, wu_ref, o_ref):
    h = h_ref[...]
    a = _dot(h, wg_ref[...])
    b = _dot(h, wu_ref[...])
    o_ref[...] = (_silu(a) * b).astype(BF16)


def _ffn_up(h, w_gate, w_up, layer, which):
    t, d = h.shape
    n_tiles = w_gate.shape[-1] // TN_UP
    wspec = pl.BlockSpec((None, None, d, TN_UP), lambda i, j: (layer, which, 0, j))
    return pl.pallas_call(
        _ffn_up_body,
        grid=(t // TM_UP, n_tiles),
        in_specs=[pl.BlockSpec((TM_UP, d), lambda i, j: (i, 0)), wspec, wspec],
        out_specs=pl.BlockSpec((None, TM_UP, TN_UP), lambda i, j: (j, i, 0)),
        out_shape=jax.ShapeDtypeStruct((n_tiles, t, TN_UP), BF16),
        compiler_params=_params(("arbitrary", "arbitrary")),
        name="ffn_up",
    )(h, w_gate, w_up)


def _down_epi_body(a_ref, w_ref, x_hbm, xtail_hbm, gate_ref, gpost_ref, *rest, coef, nk, tm, with_next,
                   head_tiles):
    if with_next:
        (gnext_ref, shift_ref, scale_ref, pshift_ref, pscale_ref, xo_hbm, ho_hbm,
         acc_ref, stat_ref, xbuf_ref, sem_ref, hbuf_ref) = rest
    else:
        xo_hbm, acc_ref, stat_ref, xbuf_ref, sem_ref = rest
    i = pl.program_id(0)
    k = pl.program_id(1)
    n_tiles = pl.num_programs(0)
    d = acc_ref.shape[-1]
    n_parts = DOWN_WB_PARTS
    part = tm // n_parts
    sem_in, sem_xout, sem_hout = 0, 1, 1 + n_parts

    def tile_rows(tile):
        return pl.ds(pl.multiple_of(tile * tm, tm), tm)

    def residual_copy(src, tile):
        return pltpu.make_async_copy(src.at[tile_rows(tile), :], xbuf_ref, sem_ref.at[sem_in])

    def residual_read(tile, action):
        if head_tiles is None:
            action(residual_copy(x_hbm, tile))
        else:
            @pl.when(tile < head_tiles)
            def _():
                action(residual_copy(x_hbm, tile))

            @pl.when(tile >= head_tiles)
            def _():
                action(residual_copy(xtail_hbm, tile - head_tiles))

    def part_rows(tile, p):
        return pl.ds(pl.multiple_of(tile * tm + p * part, part), part)

    def x_copy(tile, p):
        return pltpu.make_async_copy(xbuf_ref.at[pl.ds(p * part, part), :], xo_hbm.at[part_rows(tile, p), :],
                                     sem_ref.at[sem_xout + p])

    def h_copy(tile, p):
        return pltpu.make_async_copy(hbuf_ref, ho_hbm.at[part_rows(tile, p), :], sem_ref.at[sem_hout])

    def partial_product():
        if len(a_ref.shape) == 3:
            a = jnp.concatenate([a_ref[j] for j in range(a_ref.shape[0])], axis=1)
        else:
            a = a_ref[...]
        return _dot(a, w_ref[...])

    lanes = stat_ref.shape[-1]
    n_chunks = tm // EPI_ROWS
    part_chunks = n_chunks // n_parts

    def chunk_rows(c):
        return pl.ds(pl.multiple_of(c * EPI_ROWS, EPI_ROWS), EPI_ROWS)

    def put_stat(slot, rows, val):
        stat_ref[slot, rows, :] = jnp.broadcast_to(val, (EPI_ROWS, lanes))

    def get_stat(slot, rows):
        return jnp.tile(stat_ref[slot, rows, :], (1, d // lanes))

    def modulate_part(p, shift_r, scale_r):
        next_gain = gnext_ref[...] * (1.0 + scale_r[...])
        shift = shift_r[...]
        for c in range(part_chunks):
            rows = chunk_rows(p * part_chunks + c)
            hn = (xbuf_ref[rows, :] * get_stat(1, rows)) * next_gain + shift
            hbuf_ref[c * EPI_ROWS:(c + 1) * EPI_ROWS, :] = hn.astype(BF16)

    finishing = jnp.logical_and(jnp.logical_and(k >= 1, k <= n_parts), i > 0)

    @pl.when(k == 0)
    def _():
        acc_ref[...] = partial_product()

    @pl.when(finishing)
    def _():
        acc_ref[...] += partial_product()
        if with_next:
            @pl.when(k > 1)
            def _():
                h_copy(i - 1, k - 2).wait()

            modulate_part(k - 1, pshift_ref, pscale_ref)

    @pl.when(jnp.logical_and(jnp.logical_and(k != 0, k != nk - 1), jnp.logical_not(finishing)))
    def _():
        acc_ref[...] += partial_product()

    @pl.when(k == nk - 1)
    def _():
        y = acc_ref[...] + partial_product()
        acc_ref[...] = y
        stat_ref[0] = jnp.broadcast_to(_rstd(y), (tm, lanes))

    for p in range(n_parts):
        @pl.when(jnp.logical_and(k == p + 1, i > 0))
        def _(p=p):
            x_copy(i - 1, p).start()
            if with_next:
                h_copy(i - 1, p).start()

    @pl.when(k == n_parts + 1)
    def _():
        @pl.when(i > 0)
        def _():
            for p in range(n_parts):
                x_copy(i - 1, p).wait()
            if with_next:
                h_copy(i - 1, n_parts - 1).wait()

        residual_read(i, lambda copy: copy.start())

    @pl.when(k == nk - 1)
    def _():
        gate_gain = (coef * gate_ref[...]) * gpost_ref[...]

        def pass_residual(c, carry):
            rows = chunk_rows(c)
            xn = xbuf_ref[rows, :] + (acc_ref[rows, :] * get_stat(0, rows)) * gate_gain
            xbuf_ref[rows, :] = xn
            if with_next:
                put_stat(1, rows, _rstd(xn))
            return carry

        residual_read(i, lambda copy: copy.wait())
        lax.fori_loop(0, n_chunks, pass_residual, 0, unroll=EPI_UNROLL)

        @pl.when(i == n_tiles - 1)
        def _():
            for p in range(n_parts):
                x_copy(i, p).start()
            if with_next:
                def finish_part(p, carry):
                    modulate_part(p, shift_ref, scale_ref)
                    h_copy(i, p).start()
                    h_copy(i, p).wait()
                    return carry

                lax.fori_loop(0, n_parts, finish_part, 0)
            for p in range(n_parts):
                x_copy(i, p).wait()


def _down_epi(a, w, wsel, xs, rows, seq, mod4, gains4, layer, gate_chunk, gpost_idx, coef,
              nxt=None):
    tm = TM_DOWN
    kdim = a.shape[0] * a.shape[2] if a.ndim == 3 else a.shape[1]
    tk = next(c for c in TK_DOWN_CHOICES if kdim % c == 0 and kdim // c - 1 > DOWN_WB_PARTS + 1)
    if isinstance(xs, tuple):
        x_head, x_tail = xs
        assert x_head.shape[0] % tm == 0 and x_tail.shape[0] % tm == 0
        head_tiles = x_head.shape[0] // tm
    else:
        x_head, x_tail, head_tiles = xs, xs, None
    d = x_head.shape[1]
    if a.ndim == 3:
        per_step = tk // a.shape[2]
        assert per_step * a.shape[2] == tk and a.shape[0] % per_step == 0
        nk = a.shape[0] // per_step
        aspec = pl.BlockSpec((per_step, tm, a.shape[2]), lambda i, k: (k, i, 0))
    else:
        nk = a.shape[1] // tk
        aspec = pl.BlockSpec((tm, tk), lambda i, k: (i, k))
    assert nk - 1 > DOWN_WB_PARTS + 1 and rows % tm == 0
    assert tm % (DOWN_WB_PARTS * EPI_ROWS * EPI_UNROLL) == 0
    nlead = len(wsel)
    wspec = pl.BlockSpec((None,) * nlead + (tk, d), lambda i, k: tuple(wsel) + (k, 0))
    hbm = pl.BlockSpec(memory_space=pl.ANY)
    in_specs = [
        aspec,
        wspec,
        hbm,
        hbm,
        _mod_spec(layer, gate_chunk, d, tm, seq),
        _gain_spec(layer, gpost_idx, d),
    ]
    args = [a, w, x_head, x_tail, mod4, gains4]
    out_specs = [hbm]
    out_shape = [jax.ShapeDtypeStruct((rows, d), F32)]
    scratch = [
        pltpu.VMEM((tm, d), F32),
        pltpu.VMEM((2, tm, V7X_LANES), F32),
        pltpu.VMEM((tm, d), F32),
        pltpu.SemaphoreType.DMA((2 + DOWN_WB_PARTS,)),
    ]
    if nxt is not None:
        mod_n, gains_n, layer_n, gain_idx, shift_chunk, scale_chunk = nxt
        in_specs += [
            _gain_spec(layer_n, gain_idx, d),
            _mod_spec(layer_n, shift_chunk, d, tm, seq),
            _mod_spec(layer_n, scale_chunk, d, tm, seq),
            _mod_spec(layer_n, shift_chunk, d, tm, seq, tile_lag=1),
            _mod_spec(layer_n, scale_chunk, d, tm, seq, tile_lag=1),
        ]
        args += [gains_n, mod_n, mod_n, mod_n, mod_n]
        out_specs.append(hbm)
        out_shape.append(jax.ShapeDtypeStruct((rows, d), BF16))
        scratch.append(pltpu.VMEM((tm // DOWN_WB_PARTS, d), BF16))
    body = functools.partial(_down_epi_body, coef=coef, nk=nk, tm=tm, with_next=nxt is not None,
                             head_tiles=head_tiles)
    out = pl.pallas_call(
        body,
        grid=(rows // tm, nk),
        in_specs=in_specs,
        out_specs=out_specs,
        out_shape=out_shape,
        scratch_shapes=scratch,
        compiler_params=_params(("arbitrary", "arbitrary")),
        name="down_epi",
    )(*args)
    return (out[0], out[1]) if nxt is not None else (out[0], None)


def _in_tiles(j, lo, hi):
    return jnp.logical_and(j >= lo // TN_PROJ, j < hi // TN_PROJ)


def _proj_body(h_ref, w_ref, cos_ref, sin_ref, o_ref, *, n_lat_tiles):
    i = pl.program_id(0)
    j = pl.program_id(1)
    hd = HEAD_DIM
    acc = _dot(h_ref[...], w_ref[...].astype(BF16))
    is_q =_in_tiles(j, OFF_QA, OFF_KA) | _in_tiles(j, OFF_QB, OFF_KB) | _in_tiles(j, OFF_QC, OFF_KC)
    qscale = jnp.where(is_q, Q_SCALE, 1.0).astype(F32)
    latent = i < n_lat_tiles
    rope_full = latent & (_in_tiles(j, OFF_QB, OFF_KB) | _in_tiles(j, OFF_QC, OFF_VC))
    rope_half = latent & _in_tiles(j, OFF_KB, OFF_QC)

    def rope_heads(n_heads, scale=None):
        cos = cos_ref[...]
        sin = sin_ref[...]
        if scale is not None:
            cos = cos * scale
            sin = sin * scale
        lane = lax.broadcasted_iota(jnp.int32, cos.shape, 1)
        quarter = hd // 4
        first = (lane % (2 * quarter)) < quarter
        for hh in range(n_heads):
            a = acc[:, hh * hd:(hh + 1) * hd]
            partner = jnp.where(first, pltpu.roll(a, hd - quarter, 1), pltpu.roll(a, quarter, 1))
            o_ref[:, hh * hd:(hh + 1) * hd] = (a * cos + partner * sin).astype(BF16)

    @pl.when(rope_full)
    def _():
        rope_heads(TN_PROJ // hd, qscale)

    @pl.when(rope_half)
    def _():
        n_rope = (OFF_VB - OFF_KB) // hd
        rope_heads(n_rope)
        o_ref[:, n_rope * hd:] = acc[:, n_rope * hd:].astype(BF16)

    @pl.when(jnp.logical_not(rope_full | rope_half))
    def _():
        o_ref[...] = (acc * qscale).astype(BF16)


def _proj(h, w_in, layer, cos_t, sin_t, seq, t_lat):
    t, d = h.shape
    width = w_in.shape[-1]
    n_tiles = width // TN_PROJ
    pos_tiles = seq // TM_PROJ
    assert OFF_KB % TN_PROJ == 0 and OFF_QC - OFF_KB == TN_PROJ
    tspec = pl.BlockSpec((TM_PROJ, HEAD_DIM), lambda i, j: (i % pos_tiles, 0))
    return pl.pallas_call(
        functools.partial(_proj_body, n_lat_tiles=t_lat // TM_PROJ),
        grid=(t // TM_PROJ, n_tiles),
        in_specs=[
            pl.BlockSpec((TM_PROJ, d), lambda i, j: (i, 0)),
            pl.BlockSpec((None, d, TN_PROJ), lambda i, j: (layer, 0, j)),
            tspec, tspec,
        ],
        out_specs=pl.BlockSpec((TM_PROJ, TN_PROJ), lambda i, j: (i, j)),
        out_shape=jax.ShapeDtypeStruct((t, width), BF16),
        compiler_params=_params(("arbitrary", "arbitrary")),
        name="proj_rope",
    )(h, w_in, cos_t, sin_t)


def _rope_tables(n):
    t = jnp.arange(n, dtype=jnp.int32)
    row = (t // GRID_W).astype(F32)
    col = (t % GRID_W).astype(F32)
    half = HEAD_DIM // 2
    inv = ROPE_BASE ** (-jnp.arange(0, half, 2, dtype=F32) / half)
    ang_r = row[:, None] * inv
    ang_c = col[:, None] * inv
    cos_h = jnp.concatenate([jnp.cos(ang_r)] * 2 + [jnp.cos(ang_c)] * 2, axis=-1)
    sin_h = jnp.concatenate([-jnp.sin(ang_r), jnp.sin(ang_r), -jnp.sin(ang_c), jnp.sin(ang_c)], axis=-1)
    return cos_h, sin_h


def _na_key_row_start(blk, rows):
    return np.clip(blk * NA_Q_ROWS - NA_WIN_ROWS // 2, 0, rows - NA_KEY_ROWS)


def _na_bias_table(rpb, rows):
    kh, kw = NA_WIN_ROWS, NA_WIN_COLS
    nblk = rows // NA_Q_ROWS
    assert rows % NA_Q_ROWS == 0 and nblk >= 3 and rows >= NA_KEY_ROWS
    qc = np.arange(GRID_W)
    kc = np.arange(GRID_W)
    col_start = np.clip(qc - kw // 2, 0, GRID_W - kw)
    col_ok = (kc[None, :] >= col_start[:, None]) & (kc[None, :] < col_start[:, None] + kw)
    dc = kc[None, :] - qc[:, None] + NA_WIN_COLS - 1
    col_hot = (dc[None] == np.arange(2 * NA_WIN_COLS - 1)[:, None, None]).astype(np.float32)
    n_dr = 2 * NA_WIN_ROWS - 1
    row_hot = np.zeros((3, NA_Q_ROWS, NA_KEY_ROWS, n_dr), np.float32)
    for cls, blk in enumerate((0, 1, nblk - 1)):
        ks = _na_key_row_start(blk, rows)
        for j in range(NA_Q_ROWS):
            r = blk * NA_Q_ROWS + j
            rs = np.clip(r - kh // 2, 0, rows - kh)
            for i in range(NA_KEY_ROWS):
                if rs <= ks + i < rs + kh:
                    row_hot[cls, j, i, ks + i - r + NA_WIN_ROWS - 1] = 1.0
    b = jnp.einsum('cjir,hrd,dqk->hcjqik', row_hot, rpb.astype(F32) * LOG2E, col_hot,
                   precision=lax.Precision.HIGHEST)
    ok = (row_hot.sum(-1) > 0)[:, :, None, :, None] & col_ok[None, None, :, None, :]
    b = jnp.where(ok[None], b, NEG_INF)
    return b.reshape(rpb.shape[0], 3, NA_Q_ROWS * GRID_W, NA_KEY_ROWS * GRID_W)


def _softmax_parts(parts, extra=None):
    m = functools.reduce(jnp.maximum, [jnp.max(p, axis=-1, keepdims=True) for p in parts])
    if extra is not None:
        m = jnp.maximum(m, extra)
    es = [jnp.exp2(p - m) for p in parts]
    l = functools.reduce(jnp.add, [jnp.sum(e, axis=-1, keepdims=True) for e in es])
    if extra is not None:
        l = l + jnp.exp2(extra - m)
    inv = 1.0 / l
    return [e * inv for e in es]


def _na_body(q_ref, k_ref, v_ref, kx_ref, vx_ref, bias_ref, att_in_ref, o_ref, *, rows):
    del att_in_ref
    nblk = rows // NA_Q_ROWS
    nq = NA_Q_ROWS * GRID_W
    nkeys = NA_KEY_ROWS * GRID_W
    kx = kx_ref[...]
    vx = vx_ref[...]

    def blk_fn(bi, carry):
        ks = jnp.clip(bi * NA_Q_ROWS - NA_WIN_ROWS // 2, 0, rows - NA_KEY_ROWS)
        cls = jnp.where(bi == 0, 0, jnp.where(bi == nblk - 1, 2, 1))
        qrows = pl.ds(pl.multiple_of(bi * nq, nq), nq)
        krows = pl.ds(pl.multiple_of(ks * GRID_W, GRID_W), nkeys)
        q = q_ref[qrows, :]
        s_loc = _dot_t(q, k_ref[krows, :]) + bias_ref[cls]
        s_ctx = _dot_t(q, kx)
        p_loc, p_ctx = _softmax_parts([s_loc, s_ctx])
        o = _dot(p_loc.astype(BF16), v_ref[krows, :]) + _dot(p_ctx.astype(BF16), vx)
        o_ref[qrows, :] = o.astype(BF16)
        return carry

    lax.fori_loop(0, nblk, blk_fn, 0, unroll=NA_UNROLL)


def _na(qkv, att, bias_tbl, batch, seq, ctx_len):
    rows = seq // GRID_W
    hd = HEAD_DIM
    ctx_blk0 = batch * seq // ctx_len

    def col(off):
        return off // hd

    return pl.pallas_call(
        functools.partial(_na_body, rows=rows),
        grid=(batch, NA_HEADS),
        in_specs=[
            pl.BlockSpec((seq, hd), lambda b, h: (b, col(OFF_QA) + h)),
            pl.BlockSpec((seq, hd), lambda b, h: (b, col(OFF_KA) + h)),
            pl.BlockSpec((seq, hd), lambda b, h: (b, col(OFF_VA) + h)),
            pl.BlockSpec((ctx_len, hd), lambda b, h: (ctx_blk0 + b, col(OFF_KA) + h)),
            pl.BlockSpec((ctx_len, hd), lambda b, h: (ctx_blk0 + b, col(OFF_VA) + h)),
            pl.BlockSpec((None,) + bias_tbl.shape[1:], lambda b, h: (h, 0, 0, 0)),
            pl.BlockSpec(memory_space=pl.ANY),
        ],
        out_specs=pl.BlockSpec((seq, hd), lambda b, h: (b, col(OFF_OA) + h)),
        out_shape=jax.ShapeDtypeStruct(att.shape, att.dtype),
        input_output_aliases={6: 0},
        compiler_params=_params(("arbitrary", "arbitrary")),
        name="na_attn",
    )(qkv, qkv, qkv, qkv, qkv, bias_tbl, att)


def _sw_body(sink_ref, q_ref, k_ref, v_ref, kx_ref, vx_ref, att_in_ref, o_ref, *, seq):
    del att_in_ref
    kv = pl.program_id(1)
    blk = SW_BLOCK
    span = 3 * blk
    g = SW_GROUP
    kx = kx_ref[...]
    vx = vx_ref[...]
    grp = lax.broadcasted_iota(jnp.int32, (g * blk, 1), 0) // blk
    sink_col = jnp.zeros((g * blk, 1), F32)
    for gi in range(g):
        sink_col = jnp.where(grp == gi, sink_ref[kv * g + gi] * LOG2E, sink_col)
    qoff = lax.broadcasted_iota(jnp.int32, (g * blk, span), 0) % blk
    koff = lax.broadcasted_iota(jnp.int32, (g * blk, span), 1)

    def blk_fn(n, carry):
        start = jnp.clip((n - 1) * blk, 0, seq - span)
        qrows = pl.ds(pl.multiple_of(n * blk, blk), blk)
        krows = pl.ds(pl.multiple_of(start, blk), span)
        qb = q_ref[qrows, :]
        qs = jnp.concatenate([qb[:, gi * HEAD_DIM:(gi + 1) * HEAD_DIM] for gi in range(g)], axis=0)
        s_loc = _dot_t(qs, k_ref[krows, :])
        dist = (n * blk + qoff) - (start + koff)
        s_loc = jnp.where(jnp.abs(dist) <= SW_WINDOW, s_loc, NEG_INF)
        s_ctx = _dot_t(qs, kx)
        p_loc, p_ctx = _softmax_parts([s_loc, s_ctx], extra=sink_col)
        o = _dot(p_loc.astype(BF16), v_ref[krows, :]) + _dot(p_ctx.astype(BF16), vx)
        for gi in range(g):
            o_ref[qrows, gi * HEAD_DIM:(gi + 1) * HEAD_DIM] = o[gi * blk:(gi + 1) * blk].astype(BF16)
        return carry

    lax.fori_loop(0, seq // blk, blk_fn, 0, unroll=SW_UNROLL)


def _sw(qkv, att, sink, batch, seq, ctx_len):
    hd = HEAD_DIM
    gw = SW_GROUP * hd
    ctx_blk0 = batch * seq // ctx_len
    return pl.pallas_call(
        functools.partial(_sw_body, seq=seq),
        grid=(batch, SW_KV_HEADS),
        in_specs=[
            pl.BlockSpec(memory_space=pltpu.SMEM),
            pl.BlockSpec((seq, gw), lambda b, k: (b, OFF_QB // gw + k)),
            pl.BlockSpec((seq, hd), lambda b, k: (b, OFF_KB // hd + k)),
            pl.BlockSpec((seq, hd), lambda b, k: (b, OFF_VB // hd + k)),
            pl.BlockSpec((ctx_len, hd), lambda b, k: (ctx_blk0 + b, OFF_KB // hd + k)),
            pl.BlockSpec((ctx_len, hd), lambda b, k: (ctx_blk0 + b, OFF_VB // hd + k)),
            pl.BlockSpec(memory_space=pl.ANY),
        ],
        out_specs=pl.BlockSpec((seq, gw), lambda b, k: (b, OFF_OB // gw + k)),
        out_shape=jax.ShapeDtypeStruct(att.shape, att.dtype),
        input_output_aliases={6: 0},
        compiler_params=_params(("arbitrary", "arbitrary")),
        name="sw_attn",
    )(sink, qkv, qkv, qkv, qkv, qkv, att)


def _df_lambda(lam_ref, lambda_init):
    lv = lam_ref[...]
    s01 = jnp.sum(lv[0:1] * lv[1:2], axis=-1, keepdims=True)
    s23 = jnp.sum(lv[2:3] * lv[3:4], axis=-1, keepdims=True)
    return jnp.exp(s01) - jnp.exp(s23) + lambda_init


def _df_core(q, k_parts, v_parts, lam, subln, lambda_init):
    hd = HEAD_DIM
    exps, weights = [], []
    for c in range(2):
        qc = q[:, c * hd:(c + 1) * hd]
        scores = [_dot_t(qc, kp[:, c * hd:(c + 1) * hd]) for kp in k_parts]
        m = functools.reduce(jnp.maximum, [jnp.max(s, axis=-1, keepdims=True) for s in scores])
        es = [jnp.exp2(s - m) for s in scores]
        l = functools.reduce(jnp.add, [jnp.sum(e, axis=-1, keepdims=True) for e in es])
        exps.append(es)
        weights.append(1.0 / l)
    w0 = weights[0]
    w1 = lam * weights[1]
    o = None
    for i, vp in enumerate(v_parts):
        a = (exps[0][i] * w0 - exps[1][i] * w1).astype(BF16)
        term = _dot(a, vp[...])
        o = term if o is None else o + term
    return o * _rstd(o) * subln * (1.0 - lambda_init)


def _df_body(lam_ref, subln_ref, q_ref, k_ref, v_ref, kx_ref, vx_ref, att_in_ref, o_ref, *,
             lambda_init):
    del att_in_ref
    lam = _df_lambda(lam_ref, lambda_init)
    subln = subln_ref[...]
    for sub in range(DF_Q_STEP // DF_Q_BLOCK):
        rows = slice(sub * DF_Q_BLOCK, (sub + 1) * DF_Q_BLOCK)
        o = _df_core(q_ref[rows, :], [k_ref, kx_ref], [v_ref, vx_ref], lam, subln, lambda_init)
        o_ref[rows, :] = o.astype(BF16)


def _df(qkv, att, lam4, subln4, layer, lambda_init, batch, seq, ctx_len):
    w = 2 * HEAD_DIM
    nqb = seq // DF_Q_STEP
    ctx_blk0 = batch * seq // ctx_len
    return pl.pallas_call(
        functools.partial(_df_body, lambda_init=lambda_init),
        grid=(batch, DF_HEADS, nqb),
        in_specs=[
            pl.BlockSpec((None, 4, HEAD_DIM), lambda b, h, n: (layer, 0, 0)),
            pl.BlockSpec((None, 1, w), lambda b, h, n: (layer, 0, 0)),
            pl.BlockSpec((DF_Q_STEP, w), lambda b, h, n: (b * nqb + n, OFF_QC // w + h)),
            pl.BlockSpec((seq, w), lambda b, h, n: (b, OFF_KC // w + h)),
            pl.BlockSpec((seq, w), lambda b, h, n: (b, OFF_VC // w + h)),
            pl.BlockSpec((ctx_len, w), lambda b, h, n: (ctx_blk0 + b, OFF_KC // w + h)),
            pl.BlockSpec((ctx_len, w), lambda b, h, n: (ctx_blk0 + b, OFF_VC // w + h)),
            pl.BlockSpec(memory_space=pl.ANY),
        ],
        out_specs=pl.BlockSpec((DF_Q_STEP, w), lambda b, h, n: (b * nqb + n, OFF_OC // w + h)),
        out_shape=jax.ShapeDtypeStruct(att.shape, att.dtype),
        input_output_aliases={7: 0},
        compiler_params=_params(("arbitrary", "arbitrary", "arbitrary")),
        name="df_attn",
    )(lam4, subln4, qkv, qkv, qkv, qkv, qkv, att)


def _ctx_body(sink_ref, lam_ref, subln_ref, x_ref, att_in_ref, o_ref, *, lambda_init):
    del att_in_ref
    hd = HEAD_DIM

    def cols(off, width=hd):
        return x_ref[:, off:off + width]

    for h in range(NA_HEADS):
        s = _dot_t(cols(OFF_QA + h * hd), cols(OFF_KA + h * hd))
        (p,) = _softmax_parts([s])
        o = _dot(p.astype(BF16), cols(OFF_VA + h * hd))
        o_ref[:, OFF_OA + h * hd:OFF_OA + (h + 1) * hd] = o.astype(BF16)

    for hq in range(SW_HEADS):
        kvh = hq // SW_GROUP
        s = _dot_t(cols(OFF_QB + hq * hd), cols(OFF_KB + kvh * hd))
        sink = jnp.full((s.shape[0], 1), sink_ref[hq] * LOG2E, F32)
        (p,) = _softmax_parts([s], extra=sink)
        o = _dot(p.astype(BF16), cols(OFF_VB + kvh * hd))
        o_ref[:, OFF_OB + hq * hd:OFF_OB + (hq + 1) * hd] = o.astype(BF16)

    lam = _df_lambda(lam_ref, lambda_init)
    subln = subln_ref[...]
    for h in range(DF_HEADS):
        w = 2 * hd
        o = _df_core(cols(OFF_QC + h * w, w), [cols(OFF_KC + h * w, w)], [cols(OFF_VC + h * w, w)],
                     lam, subln, lambda_init)
        o_ref[:, OFF_OC + h * w:OFF_OC + (h + 1) * w] = o.astype(BF16)


def _ctx_attn(qkv, att, sink, lam4, subln4, layer, lambda_init, batch, seq, ctx_len):
    ctx_blk0 = batch * seq // ctx_len
    return pl.pallas_call(
        functools.partial(_ctx_body, lambda_init=lambda_init),
        grid=(batch,),
        in_specs=[
            pl.BlockSpec(memory_space=pltpu.SMEM),
            pl.BlockSpec((None, 4, HEAD_DIM), lambda b: (layer, 0, 0)),
            pl.BlockSpec((None, 1, 2 * HEAD_DIM), lambda b: (layer, 0, 0)),
            pl.BlockSpec((ctx_len, IN_WIDTH), lambda b: (ctx_blk0 + b, 0)),
            pl.BlockSpec(memory_space=pl.ANY),
        ],
        out_specs=pl.BlockSpec((ctx_len, ATT_WIDTH), lambda b: (ctx_blk0 + b, 0)),
        out_shape=jax.ShapeDtypeStruct(att.shape, att.dtype),
        input_output_aliases={4: 0},
        compiler_params=_params(("arbitrary",)),
        name="ctx_attn",
    )(sink, lam4, subln4, qkv, att)


def _merge_body(h_ref, att_ref, wg_ref, bg_ref, wb_ref, o_ref, *, branch_width):
    h = h_ref[...]
    acc = None
    for i in range(3):
        gate = jax.nn.sigmoid(_dot(h, wg_ref[i]) + bg_ref[i])
        br = _dot(att_ref[:, i * branch_width:(i + 1) * branch_width], wb_ref[i])
        acc = gate * br if acc is None else acc + gate * br
    o_ref[...] = acc.astype(BF16)


def _merge(h, att, w_gate, b_gate4, w_branch, layer, rows):
    d = h.shape[1]
    nb, bw = w_branch.shape[1], w_branch.shape[2]
    tm, tn = TM_MERGE, TN_MERGE
    return pl.pallas_call(
        functools.partial(_merge_body, branch_width=bw),
        grid=(rows // tm, d // tn),
        in_specs=[
            pl.BlockSpec((tm, d), lambda i, j: (i, 0)),
            pl.BlockSpec((tm, nb * bw), lambda i, j: (i, 0)),
            pl.BlockSpec((None, nb, d, tn), lambda i, j: (layer, 0, 0, j)),
            pl.BlockSpec((None, nb, 1, tn), lambda i, j: (layer, 0, 0, j)),
            pl.BlockSpec((None, nb, bw, tn), lambda i, j: (layer, 0, 0, j)),
        ],
        out_specs=pl.BlockSpec((tm, tn), lambda i, j: (i, j)),
        out_shape=jax.ShapeDtypeStruct((rows, d), BF16),
        compiler_params=_params(("arbitrary", "arbitrary")),
        name="merge",
    )(h, att, w_gate, b_gate4, w_branch)


@jax.jit
def _forward(x, c, ctx, c_ctx, w_ada, b_ada, norm_g, w_ffn_gate, w_ffn_up, w_ffn_down,
             w_in, na_rpb, sw_sink, df_lambda, df_subln_g, w_branch, w_gate, b_gate, w_out):
    batch, seq, d = x.shape
    ctx_len = ctx.shape[1]
    depth = w_ada.shape[0]
    t_lat = batch * seq
    t = t_lat + batch * ctx_len
    assert seq % TM_UP == 0 and (batch * ctx_len) % TM_UP == 0 and batch + 1 <= MOD_ROWS
    assert seq % ctx_len == 0 and ctx_len % SW_BLOCK == 0

    xs = (x.reshape(t_lat, d), ctx.reshape(batch * ctx_len, d))
    cc = jnp.zeros((MOD_ROWS, d), F32).at[:batch].set(c).at[batch].set(c_ctx)
    mod4 = _adaln(cc, w_ada, b_ada).reshape(depth, MOD_ROWS, 1, N_MOD * d)
    gains4 = norm_g.reshape(depth, norm_g.shape[1], 1, d)

    wg16 = w_ffn_gate.astype(BF16)
    wu16 = w_ffn_up.astype(BF16)
    wd16 = w_ffn_down.astype(BF16)
    wbr16 = w_branch.astype(BF16)
    wgt16 = w_gate.astype(BF16)
    wout16 = w_out.astype(BF16)
    b_gate4 = b_gate.reshape(depth, b_gate.shape[1], 1, d)
    lam4 = df_lambda.astype(F32)
    subln4 = df_subln_g.reshape(depth, 1, -1)
    cos_t, sin_t = _rope_tables(seq)

    h = _norm_mod(xs[0], xs[1], mod4, gains4, 0, seq)
    for l in range(depth):
        last = l == depth - 1
        lambda_init = 0.8 - 0.6 * math.exp(-0.3 * l)
        u = _ffn_up(h, wg16, wu16, l, 0)
        xs, h = _down_epi(u, wd16, (l, 0), xs, t, seq, mod4, gains4, l, 2, 1, 0.5,
                          nxt=(mod4, gains4, l, 2, 3, 4))
        qkv = _proj(h, w_in, l, cos_t, sin_t, seq, t_lat)
        rows = t_lat if last else t
        att = jnp.zeros((rows, ATT_WIDTH), BF16)
        att = _na(qkv, att, _na_bias_table(na_rpb[l], seq // GRID_W), batch, seq, ctx_len)
        att = _sw(qkv, att, sw_sink[l], batch, seq, ctx_len)
        att = _df(qkv, att, lam4, subln4, l, lambda_init, batch, seq, ctx_len)
        if not last:
            att = _ctx_attn(qkv, att, sw_sink[l], lam4, subln4, l, lambda_init, batch, seq, ctx_len)
        merged = _merge(h, att, wgt16, b_gate4, wbr16, l, rows)
        xs, h = _down_epi(merged, wout16, (l,), xs, rows, seq, mod4, gains4, l, 5, 3, 1.0,
                          nxt=(mod4, gains4, l, 4, 6, 7))
        u = _ffn_up(h, wg16, wu16, l, 1)
        nxt = None if last else (mod4, gains4, l + 1, 0, 0, 1)
        xs, h = _down_epi(u, wd16, (l, 1), xs, rows, seq, mod4, gains4, l, 8, 5, 0.5, nxt=nxt)
    return xs.reshape(batch, seq, d)


def kernel(x, c, ctx, c_ctx, w_ada, b_ada, norm_g, w_ffn_gate, w_ffn_up, w_ffn_down,
           w_in, na_rpb, sw_sink, df_lambda, df_subln_g, w_branch, w_gate, b_gate, w_out):
    return _forward(x, c, ctx, c_ctx, w_ada, b_ada, norm_g, w_ffn_gate, w_ffn_up, w_ffn_down,
                    w_in, na_rpb, sw_sink, df_lambda, df_subln_g, w_branch, w_gate, b_gate, w_out)
```

```python
import functools
import math

import numpy as np
import jax
import jax.numpy as jnp
from jax import lax
from jax.experimental import pallas as pl
from jax.experimental.pallas import tpu as pltpu

F32 = jnp.float32
BF16 = jnp.bfloat16

GRID_W = 64
HEAD_DIM = 128
NA_HEADS = 8
NA_WIN_ROWS = 8
NA_WIN_COLS = 16
SW_HEADS = 8
SW_KV_HEADS = 2
SW_GROUP = SW_HEADS // SW_KV_HEADS
SW_WINDOW = 128
SW_BLOCK = 128
DF_HEADS = 4
DF_Q_BLOCK = 128
N_MOD = 9
ROPE_BASE = 10000.0
EPS = 1e-6
NEG_INF = -1e30
LOG2E = math.log2(math.e)
Q_SCALE = HEAD_DIM ** -0.5 * LOG2E

OFF_QA, OFF_KA, OFF_VA = 0, 1024, 2048
OFF_QB, OFF_KB, OFF_VB = 3072, 4096, 4352
OFF_QC, OFF_KC, OFF_VC = 4608, 5632, 6656
IN_WIDTH = 7680
OFF_OA, OFF_OB, OFF_OC = 0, 1024, 2048
ATT_WIDTH = 3072

V7X_LANES = 128
MOD_ROWS = 8
V7X_VMEM_LIMIT = 58 * 1024 * 1024

TM_UP, TN_UP = 1024, 512
TM_DOWN = 1024
TK_DOWN_CHOICES = (1024, 512)
DOWN_WB_PARTS = 4
TM_PROJ, TN_PROJ = 1024, 512
DF_Q_STEP = 16 * DF_Q_BLOCK
NA_Q_ROWS = 4
NA_KEY_ROWS = 12
NA_UNROLL = 4
SW_UNROLL = 4
TM_MERGE, TN_MERGE = 1024, 256
TN_ADA = 1024
TM_NORM = 512
EPI_ROWS = 16
EPI_UNROLL = 8


def _params(sem):
    return pltpu.CompilerParams(dimension_semantics=sem, vmem_limit_bytes=V7X_VMEM_LIMIT)


def _dot(a, b):
    return jnp.dot(a, b, preferred_element_type=F32)


def _dot_t(a, b):
    return lax.dot_general(a, b, (((1,), (1,)), ((), ())), preferred_element_type=F32)


def _rstd(y):
    return lax.rsqrt(jnp.mean(y * y, axis=-1, keepdims=True) + EPS)


def _silu(a):
    return a * jax.nn.sigmoid(a)


def _adaln_body(c_ref, w_ref, b_ref, o_ref):
    s = _silu(c_ref[...]).astype(BF16)
    o_ref[...] = _dot(s, w_ref[...].astype(BF16)) + b_ref[...]


def _adaln(cc, w_ada, b_ada):
    depth, d, nd = w_ada.shape
    return pl.pallas_call(
        _adaln_body,
        grid=(depth, nd // TN_ADA),
        in_specs=[
            pl.BlockSpec((MOD_ROWS, d), lambda l, j: (0, 0)),
            pl.BlockSpec((None, d, TN_ADA), lambda l, j: (l, 0, j)),
            pl.BlockSpec((None, 1, TN_ADA), lambda l, j: (l, 0, j)),
        ],
        out_specs=pl.BlockSpec((None, MOD_ROWS, TN_ADA), lambda l, j: (l, 0, j)),
        out_shape=jax.ShapeDtypeStruct((depth, MOD_ROWS, nd), F32),
        compiler_params=_params(("arbitrary", "arbitrary")),
        name="adaln",
    )(cc, w_ada, b_ada.reshape(depth, 1, nd))


def _mod_spec(layer, chunk, d, tm, seq, tile_lag=0):
    return pl.BlockSpec((None, None, 1, d),
                        lambda i, *_: (layer, (jnp.maximum(i - tile_lag, 0) * tm) // seq, 0, chunk))


def _gain_spec(layer, idx, d):
    return pl.BlockSpec((None, None, 1, d), lambda i, *_: (layer, idx, 0, 0))


def _norm_mod_body(x_ref, xtail_ref, g_ref, shift_ref, scale_ref, o_ref, *, head_tiles):
    def emit(src_ref):
        x = src_ref[...]
        y = x * _rstd(x) * g_ref[...]
        o_ref[...] = (y * (1.0 + scale_ref[...]) + shift_ref[...]).astype(BF16)

    i = pl.program_id(0)

    @pl.when(i < head_tiles)
    def _():
        emit(x_ref)

    @pl.when(i >= head_tiles)
    def _():
        emit(xtail_ref)


def _norm_mod(x_head, x_tail, mod4, gains4, layer, seq):
    d = x_head.shape[1]
    tm = TM_NORM
    assert x_head.shape[0] % tm == 0 and x_tail.shape[0] % tm == 0
    head_tiles = x_head.shape[0] // tm
    tail_tiles = x_tail.shape[0] // tm
    return pl.pallas_call(
        functools.partial(_norm_mod_body, head_tiles=head_tiles),
        grid=(head_tiles + tail_tiles,),
        in_specs=[
            pl.BlockSpec((tm, d), lambda i: (jnp.minimum(i, head_tiles - 1), 0)),
            pl.BlockSpec((tm, d), lambda i: (jnp.maximum(i - head_tiles, 0), 0)),
            _gain_spec(layer, 0, d),
            _mod_spec(layer, 0, d, tm, seq),
            _mod_spec(layer, 1, d, tm, seq),
        ],
        out_specs=pl.BlockSpec((tm, d), lambda i: (i, 0)),
        out_shape=jax.ShapeDtypeStruct(((head_tiles + tail_tiles) * tm, d), BF16),
        compiler_params=_params(("arbitrary",)),
        name="norm_mod",
    )(x_head, x_tail, gains4, mod4, mod4)


def _ffn_up_body(h_ref, wg_ref, wu_ref, o_ref):
    h = h_ref[...]
    a = _dot(h, wg_ref[...])
    b = _dot(h, wu_ref[...])
    o_ref[...] = (_silu(a) * b).astype(BF16)


def _ffn_up(h, w_gate, w_up, layer, which):
    t, d = h.shape
    n_tiles = w_gate.shape[-1] // TN_UP
    wspec = pl.BlockSpec((None, None, d, TN_UP), lambda i, j: (layer, which, 0, j))
    return pl.pallas_call(
        _ffn_up_body,
        grid=(t // TM_UP, n_tiles),
        in_specs=[pl.BlockSpec((TM_UP, d), lambda i, j: (i, 0)), wspec, wspec],
        out_specs=pl.BlockSpec((None, TM_UP, TN_UP), lambda i, j: (j, i, 0)),
        out_shape=jax.ShapeDtypeStruct((n_tiles, t, TN_UP), BF16),
        compiler_params=_params(("arbitrary", "arbitrary")),
        name="ffn_up",
    )(h, w_gate, w_up)


def _down_epi_body(a_ref, w_ref, x_hbm, xtail_hbm, gate_ref, gpost_ref, *rest, coef, nk, tm, with_next,
                   head_tiles):
    if with_next:
        (gnext_ref, shift_ref, scale_ref, pshift_ref, pscale_ref, xo_hbm, ho_hbm,
         acc_ref, stat_ref, xbuf_ref, sem_ref, hbuf_ref) = rest
    else:
        xo_hbm, acc_ref, stat_ref, xbuf_ref, sem_ref = rest
    i = pl.program_id(0)
    k = pl.program_id(1)
    n_tiles = pl.num_programs(0)
    d = acc_ref.shape[-1]
    n_parts = DOWN_WB_PARTS
    part = tm // n_parts
    sem_in, sem_xout, sem_hout = 0, 1, 1 + n_parts

    def tile_rows(tile):
        return pl.ds(pl.multiple_of(tile * tm, tm), tm)

    def residual_copy(src, tile):
        return pltpu.make_async_copy(src.at[tile_rows(tile), :], xbuf_ref, sem_ref.at[sem_in])

    def residual_read(tile, action):
        if head_tiles is None:
            action(residual_copy(x_hbm, tile))
        else:
            @pl.when(tile < head_tiles)
            def _():
                action(residual_copy(x_hbm, tile))

            @pl.when(tile >= head_tiles)
            def _():
                action(residual_copy(xtail_hbm, tile - head_tiles))

    def part_rows(tile, p):
        return pl.ds(pl.multiple_of(tile * tm + p * part, part), part)

    def x_copy(tile, p):
        return pltpu.make_async_copy(xbuf_ref.at[pl.ds(p * part, part), :], xo_hbm.at[part_rows(tile, p), :],
                                     sem_ref.at[sem_xout + p])

    def h_copy(tile, p):
        return pltpu.make_async_copy(hbuf_ref, ho_hbm.at[part_rows(tile, p), :], sem_ref.at[sem_hout])

    def partial_product():
        if len(a_ref.shape) == 3:
            a = jnp.concatenate([a_ref[j] for j in range(a_ref.shape[0])], axis=1)
        else:
            a = a_ref[...]
        return _dot(a, w_ref[...])

    lanes = stat_ref.shape[-1]
    n_chunks = tm // EPI_ROWS
    part_chunks = n_chunks // n_parts

    def chunk_rows(c):
        return pl.ds(pl.multiple_of(c * EPI_ROWS, EPI_ROWS), EPI_ROWS)

    def put_stat(slot, rows, val):
        stat_ref[slot, rows, :] = jnp.broadcast_to(val, (EPI_ROWS, lanes))

    def get_stat(slot, rows):
        return jnp.tile(stat_ref[slot, rows, :], (1, d // lanes))

    def modulate_part(p, shift_r, scale_r):
        next_gain = gnext_ref[...] * (1.0 + scale_r[...])
        shift = shift_r[...]
        for c in range(part_chunks):
            rows = chunk_rows(p * part_chunks + c)
            hn = (xbuf_ref[rows, :] * get_stat(1, rows)) * next_gain + shift
            hbuf_ref[c * EPI_ROWS:(c + 1) * EPI_ROWS, :] = hn.astype(BF16)

    finishing = jnp.logical_and(jnp.logical_and(k >= 1, k <= n_parts), i > 0)

    @pl.when(k == 0)
    def _():
        acc_ref[...] = partial_product()

    @pl.when(finishing)
    def _():
        acc_ref[...] += partial_product()
        if with_next:
            @pl.when(k > 1)
            def _():
                h_copy(i - 1, k - 2).wait()

            modulate_part(k - 1, pshift_ref, pscale_ref)

    @pl.when(jnp.logical_and(jnp.logical_and(k != 0, k != nk - 1), jnp.logical_not(finishing)))
    def _():
        acc_ref[...] += partial_product()

    @pl.when(k == nk - 1)
    def _():
        y = acc_ref[...] + partial_product()
        acc_ref[...] = y
        stat_ref[0] = jnp.broadcast_to(_rstd(y), (tm, lanes))

    for p in range(n_parts):
        @pl.when(jnp.logical_and(k == p + 1, i > 0))
        def _(p=p):
            x_copy(i - 1, p).start()
            if with_next:
                h_copy(i - 1, p).start()

    @pl.when(k == n_parts + 1)
    def _():
        @pl.when(i > 0)
        def _():
            for p in range(n_parts):
                x_copy(i - 1, p).wait()
            if with_next:
                h_copy(i - 1, n_parts - 1).wait()

        residual_read(i, lambda copy: copy.start())

    @pl.when(k == nk - 1)
    def _():
        gate_gain = (coef * gate_ref[...]) * gpost_ref[...]

        def pass_residual(c, carry):
            rows = chunk_rows(c)
            xn = xbuf_ref[rows, :] + (acc_ref[rows, :] * get_stat(0, rows)) * gate_gain
            xbuf_ref[rows, :] = xn
            if with_next:
                put_stat(1, rows, _rstd(xn))
            return carry

        residual_read(i, lambda copy: copy.wait())
        lax.fori_loop(0, n_chunks, pass_residual, 0, unroll=EPI_UNROLL)

        @pl.when(i == n_tiles - 1)
        def _():
            for p in range(n_parts):
                x_copy(i, p).start()
            if with_next:
                def finish_part(p, carry):
                    modulate_part(p, shift_ref, scale_ref)
                    h_copy(i, p).start()
                    h_copy(i, p).wait()
                    return carry

                lax.fori_loop(0, n_parts, finish_part, 0)
            for p in range(n_parts):
                x_copy(i, p).wait()


def _down_epi(a, w, wsel, xs, rows, seq, mod4, gains4, layer, gate_chunk, gpost_idx, coef,
              nxt=None):
    tm = TM_DOWN
    kdim = a.shape[0] * a.shape[2] if a.ndim == 3 else a.shape[1]
    tk = next(c for c in TK_DOWN_CHOICES if kdim % c == 0 and kdim // c - 1 > DOWN_WB_PARTS + 1)
    if isinstance(xs, tuple):
        x_head, x_tail = xs
        assert x_head.shape[0] % tm == 0 and x_tail.shape[0] % tm == 0
        head_tiles = x_head.shape[0] // tm
    else:
        x_head, x_tail, head_tiles = xs, xs, None
    d = x_head.shape[1]
    if a.ndim == 3:
        per_step = tk // a.shape[2]
        assert per_step * a.shape[2] == tk and a.shape[0] % per_step == 0
        nk = a.shape[0] // per_step
        aspec = pl.BlockSpec((per_step, tm, a.shape[2]), lambda i, k: (k, i, 0))
    else:
        nk = a.shape[1] // tk
        aspec = pl.BlockSpec((tm, tk), lambda i, k: (i, k))
    assert nk - 1 > DOWN_WB_PARTS + 1 and rows % tm == 0
    assert tm % (DOWN_WB_PARTS * EPI_ROWS * EPI_UNROLL) == 0
    nlead = len(wsel)
    wspec = pl.BlockSpec((None,) * nlead + (tk, d), lambda i, k: tuple(wsel) + (k, 0))
    hbm = pl.BlockSpec(memory_space=pl.ANY)
    in_specs = [
        aspec,
        wspec,
        hbm,
        hbm,
        _mod_spec(layer, gate_chunk, d, tm, seq),
        _gain_spec(layer, gpost_idx, d),
    ]
    args = [a, w, x_head, x_tail, mod4, gains4]
    out_specs = [hbm]
    out_shape = [jax.ShapeDtypeStruct((rows, d), F32)]
    scratch = [
        pltpu.VMEM((tm, d), F32),
        pltpu.VMEM((2, tm, V7X_LANES), F32),
        pltpu.VMEM((tm, d), F32),
        pltpu.SemaphoreType.DMA((2 + DOWN_WB_PARTS,)),
    ]
    if nxt is not None:
        mod_n, gains_n, layer_n, gain_idx, shift_chunk, scale_chunk = nxt
        in_specs += [
            _gain_spec(layer_n, gain_idx, d),
            _mod_spec(layer_n, shift_chunk, d, tm, seq),
            _mod_spec(layer_n, scale_chunk, d, tm, seq),
            _mod_spec(layer_n, shift_chunk, d, tm, seq, tile_lag=1),
            _mod_spec(layer_n, scale_chunk, d, tm, seq, tile_lag=1),
        ]
        args += [gains_n, mod_n, mod_n, mod_n, mod_n]
        out_specs.append(hbm)
        out_shape.append(jax.ShapeDtypeStruct((rows, d), BF16))
        scratch.append(pltpu.VMEM((tm // DOWN_WB_PARTS, d), BF16))
    body = functools.partial(_down_epi_body, coef=coef, nk=nk, tm=tm, with_next=nxt is not None,
                             head_tiles=head_tiles)
    out = pl.pallas_call(
        body,
        grid=(rows // tm, nk),
        in_specs=in_specs,
        out_specs=out_specs,
        out_shape=out_shape,
        scratch_shapes=scratch,
        compiler_params=_params(("arbitrary", "arbitrary")),
        name="down_epi",
    )(*args)
    return (out[0], out[1]) if nxt is not None else (out[0], None)


def _in_tiles(j, lo, hi):
    return jnp.logical_and(j >= lo // TN_PROJ, j < hi // TN_PROJ)


def _proj_body(h_ref, w_ref, cos_ref, sin_ref, o_ref, *, n_lat_tiles):
    i = pl.program_id(0)
    j = pl.program_id(1)
    hd = HEAD_DIM
    acc = _dot(h_ref[...], w_ref[...].astype(BF16))
    is_q =_in_tiles(j, OFF_QA, OFF_KA) | _in_tiles(j, OFF_QB, OFF_KB) | _in_tiles(j, OFF_QC, OFF_KC)
    qscale = jnp.where(is_q, Q_SCALE, 1.0).astype(F32)
    latent = i < n_lat_tiles
    rope_full = latent & (_in_tiles(j, OFF_QB, OFF_KB) | _in_tiles(j, OFF_QC, OFF_VC))
    rope_half = latent & _in_tiles(j, OFF_KB, OFF_QC)

    def rope_heads(n_heads, scale=None):
        cos = cos_ref[...]
        sin = sin_ref[...]
        if scale is not None:
            cos = cos * scale
            sin = sin * scale
        lane = lax.broadcasted_iota(jnp.int32, cos.shape, 1)
        quarter = hd // 4
        first = (lane % (2 * quarter)) < quarter
        for hh in range(n_heads):
            a = acc[:, hh * hd:(hh + 1) * hd]
            partner = jnp.where(first, pltpu.roll(a, hd - quarter, 1), pltpu.roll(a, quarter, 1))
            o_ref[:, hh * hd:(hh + 1) * hd] = (a * cos + partner * sin).astype(BF16)

    @pl.when(rope_full)
    def _():
        rope_heads(TN_PROJ // hd, qscale)

    @pl.when(rope_half)
    def _():
        n_rope = (OFF_VB - OFF_KB) // hd
        rope_heads(n_rope)
        o_ref[:, n_rope * hd:] = acc[:, n_rope * hd:].astype(BF16)

    @pl.when(jnp.logical_not(rope_full | rope_half))
    def _():
        o_ref[...] = (acc * qscale).astype(BF16)


def _proj(h, w_in, layer, cos_t, sin_t, seq, t_lat):
    t, d = h.shape
    width = w_in.shape[-1]
    n_tiles = width // TN_PROJ
    pos_tiles = seq // TM_PROJ
    assert OFF_KB % TN_PROJ == 0 and OFF_QC - OFF_KB == TN_PROJ
    tspec = pl.BlockSpec((TM_PROJ, HEAD_DIM), lambda i, j: (i % pos_tiles, 0))
    return pl.pallas_call(
        functools.partial(_proj_body, n_lat_tiles=t_lat // TM_PROJ),
        grid=(t // TM_PROJ, n_tiles),
        in_specs=[
            pl.BlockSpec((TM_PROJ, d), lambda i, j: (i, 0)),
            pl.BlockSpec((None, d, TN_PROJ), lambda i, j: (layer, 0, j)),
            tspec, tspec,
        ],
        out_specs=pl.BlockSpec((TM_PROJ, TN_PROJ), lambda i, j: (i, j)),
        out_shape=jax.ShapeDtypeStruct((t, width), BF16),
        compiler_params=_params(("arbitrary", "arbitrary")),
        name="proj_rope",
    )(h, w_in, cos_t, sin_t)


def _rope_tables(n):
    t = jnp.arange(n, dtype=jnp.int32)
    row = (t // GRID_W).astype(F32)
    col = (t % GRID_W).astype(F32)
    half = HEAD_DIM // 2
    inv = ROPE_BASE ** (-jnp.arange(0, half, 2, dtype=F32) / half)
    ang_r = row[:, None] * inv
    ang_c = col[:, None] * inv
    cos_h = jnp.concatenate([jnp.cos(ang_r)] * 2 + [jnp.cos(ang_c)] * 2, axis=-1)
    sin_h = jnp.concatenate([-jnp.sin(ang_r), jnp.sin(ang_r), -jnp.sin(ang_c), jnp.sin(ang_c)], axis=-1)
    return cos_h, sin_h


def _na_key_row_start(blk, rows):
    return np.clip(blk * NA_Q_ROWS - NA_WIN_ROWS // 2, 0, rows - NA_KEY_ROWS)


def _na_bias_table(rpb, rows):
    kh, kw = NA_WIN_ROWS, NA_WIN_COLS
    nblk = rows // NA_Q_ROWS
    assert rows % NA_Q_ROWS == 0 and nblk >= 3 and rows >= NA_KEY_ROWS
    qc = np.arange(GRID_W)
    kc = np.arange(GRID_W)
    col_start = np.clip(qc - kw // 2, 0, GRID_W - kw)
    col_ok = (kc[None, :] >= col_start[:, None]) & (kc[None, :] < col_start[:, None] + kw)
    dc = kc[None, :] - qc[:, None] + NA_WIN_COLS - 1
    col_hot = (dc[None] == np.arange(2 * NA_WIN_COLS - 1)[:, None, None]).astype(np.float32)
    n_dr = 2 * NA_WIN_ROWS - 1
    row_hot = np.zeros((3, NA_Q_ROWS, NA_KEY_ROWS, n_dr), np.float32)
    for cls, blk in enumerate((0, 1, nblk - 1)):
        ks = _na_key_row_start(blk, rows)
        for j in range(NA_Q_ROWS):
            r = blk * NA_Q_ROWS + j
            rs = np.clip(r - kh // 2, 0, rows - kh)
            for i in range(NA_KEY_ROWS):
                if rs <= ks + i < rs + kh:
                    row_hot[cls, j, i, ks + i - r + NA_WIN_ROWS - 1] = 1.0
    b = jnp.einsum('cjir,hrd,dqk->hcjqik', row_hot, rpb.astype(F32) * LOG2E, col_hot,
                   precision=lax.Precision.HIGHEST)
    ok = (row_hot.sum(-1) > 0)[:, :, None, :, None] & col_ok[None, None, :, None, :]
    b = jnp.where(ok[None], b, NEG_INF)
    return b.reshape(rpb.shape[0], 3, NA_Q_ROWS * GRID_W, NA_KEY_ROWS * GRID_W)


def _softmax_parts(parts, extra=None):
    m = functools.reduce(jnp.maximum, [jnp.max(p, axis=-1, keepdims=True) for p in parts])
    if extra is not None:
        m = jnp.maximum(m, extra)
    es = [jnp.exp2(p - m) for p in parts]
    l = functools.reduce(jnp.add, [jnp.sum(e, axis=-1, keepdims=True) for e in es])
    if extra is not None:
        l = l + jnp.exp2(extra - m)
    inv = 1.0 / l
    return [e * inv for e in es]


def _na_body(q_ref, k_ref, v_ref, kx_ref, vx_ref, bias_ref, att_in_ref, o_ref, *, rows):
    del att_in_ref
    nblk = rows // NA_Q_ROWS
    nq = NA_Q_ROWS * GRID_W
    nkeys = NA_KEY_ROWS * GRID_W
    kx = kx_ref[...]
    vx = vx_ref[...]

    def blk_fn(bi, carry):
        ks = jnp.clip(bi * NA_Q_ROWS - NA_WIN_ROWS // 2, 0, rows - NA_KEY_ROWS)
        cls = jnp.where(bi == 0, 0, jnp.where(bi == nblk - 1, 2, 1))
        qrows = pl.ds(pl.multiple_of(bi * nq, nq), nq)
        krows = pl.ds(pl.multiple_of(ks * GRID_W, GRID_W), nkeys)
        q = q_ref[qrows, :]
        s_loc = _dot_t(q, k_ref[krows, :]) + bias_ref[cls]
        s_ctx = _dot_t(q, kx)
        p_loc, p_ctx = _softmax_parts([s_loc, s_ctx])
        o = _dot(p_loc.astype(BF16), v_ref[krows, :]) + _dot(p_ctx.astype(BF16), vx)
        o_ref[qrows, :] = o.astype(BF16)
        return carry

    lax.fori_loop(0, nblk, blk_fn, 0, unroll=NA_UNROLL)


def _na(qkv, att, bias_tbl, batch, seq, ctx_len):
    rows = seq // GRID_W
    hd = HEAD_DIM
    ctx_blk0 = batch * seq // ctx_len

    def col(off):
        return off // hd

    return pl.pallas_call(
        functools.partial(_na_body, rows=rows),
        grid=(batch, NA_HEADS),
        in_specs=[
            pl.BlockSpec((seq, hd), lambda b, h: (b, col(OFF_QA) + h)),
            pl.BlockSpec((seq, hd), lambda b, h: (b, col(OFF_KA) + h)),
            pl.BlockSpec((seq, hd), lambda b, h: (b, col(OFF_VA) + h)),
            pl.BlockSpec((ctx_len, hd), lambda b, h: (ctx_blk0 + b, col(OFF_KA) + h)),
            pl.BlockSpec((ctx_len, hd), lambda b, h: (ctx_blk0 + b, col(OFF_VA) + h)),
            pl.BlockSpec((None,) + bias_tbl.shape[1:], lambda b, h: (h, 0, 0, 0)),
            pl.BlockSpec(memory_space=pl.ANY),
        ],
        out_specs=pl.BlockSpec((seq, hd), lambda b, h: (b, col(OFF_OA) + h)),
        out_shape=jax.ShapeDtypeStruct(att.shape, att.dtype),
        input_output_aliases={6: 0},
        compiler_params=_params(("arbitrary", "arbitrary")),
        name="na_attn",
    )(qkv, qkv, qkv, qkv, qkv, bias_tbl, att)


def _sw_body(sink_ref, q_ref, k_ref, v_ref, kx_ref, vx_ref, att_in_ref, o_ref, *, seq):
    del att_in_ref
    kv = pl.program_id(1)
    blk = SW_BLOCK
    span = 3 * blk
    g = SW_GROUP
    kx = kx_ref[...]
    vx = vx_ref[...]
    grp = lax.broadcasted_iota(jnp.int32, (g * blk, 1), 0) // blk
    sink_col = jnp.zeros((g * blk, 1), F32)
    for gi in range(g):
        sink_col = jnp.where(grp == gi, sink_ref[kv * g + gi] * LOG2E, sink_col)
    qoff = lax.broadcasted_iota(jnp.int32, (g * blk, span), 0) % blk
    koff = lax.broadcasted_iota(jnp.int32, (g * blk, span), 1)

    def blk_fn(n, carry):
        start = jnp.clip((n - 1) * blk, 0, seq - span)
        qrows = pl.ds(pl.multiple_of(n * blk, blk), blk)
        krows = pl.ds(pl.multiple_of(start, blk), span)
        qb = q_ref[qrows, :]
        qs = jnp.concatenate([qb[:, gi * HEAD_DIM:(gi + 1) * HEAD_DIM] for gi in range(g)], axis=0)
        s_loc = _dot_t(qs, k_ref[krows, :])
        dist = (n * blk + qoff) - (start + koff)
        s_loc = jnp.where(jnp.abs(dist) <= SW_WINDOW, s_loc, NEG_INF)
        s_ctx = _dot_t(qs, kx)
        p_loc, p_ctx = _softmax_parts([s_loc, s_ctx], extra=sink_col)
        o = _dot(p_loc.astype(BF16), v_ref[krows, :]) + _dot(p_ctx.astype(BF16), vx)
        for gi in range(g):
            o_ref[qrows, gi * HEAD_DIM:(gi + 1) * HEAD_DIM] = o[gi * blk:(gi + 1) * blk].astype(BF16)
        return carry

    lax.fori_loop(0, seq // blk, blk_fn, 0, unroll=SW_UNROLL)


def _sw(qkv, att, sink, batch, seq, ctx_len):
    hd = HEAD_DIM
    gw = SW_GROUP * hd
    ctx_blk0 = batch * seq // ctx_len
    return pl.pallas_call(
        functools.partial(_sw_body, seq=seq),
        grid=(batch, SW_KV_HEADS),
        in_specs=[
            pl.BlockSpec(memory_space=pltpu.SMEM),
            pl.BlockSpec((seq, gw), lambda b, k: (b, OFF_QB // gw + k)),
            pl.BlockSpec((seq, hd), lambda b, k: (b, OFF_KB // hd + k)),
            pl.BlockSpec((seq, hd), lambda b, k: (b, OFF_VB // hd + k)),
            pl.BlockSpec((ctx_len, hd), lambda b, k: (ctx_blk0 + b, OFF_KB // hd + k)),
            pl.BlockSpec((ctx_len, hd), lambda b, k: (ctx_blk0 + b, OFF_VB // hd + k)),
            pl.BlockSpec(memory_space=pl.ANY),
        ],
        out_specs=pl.BlockSpec((seq, gw), lambda b, k: (b, OFF_OB // gw + k)),
        out_shape=jax.ShapeDtypeStruct(att.shape, att.dtype),
        input_output_aliases={6: 0},
        compiler_params=_params(("arbitrary", "arbitrary")),
        name="sw_attn",
    )(sink, qkv, qkv, qkv, qkv, qkv, att)


def _df_lambda(lam_ref, lambda_init):
    lv = lam_ref[...]
    s01 = jnp.sum(lv[0:1] * lv[1:2], axis=-1, keepdims=True)
    s23 = jnp.sum(lv[2:3] * lv[3:4], axis=-1, keepdims=True)
    return jnp.exp(s01) - jnp.exp(s23) + lambda_init


def _df_core(q, k_parts, v_parts, lam, subln, lambda_init):
    hd = HEAD_DIM
    exps, weights = [], []
    for c in range(2):
        qc = q[:, c * hd:(c + 1) * hd]
        scores = [_dot_t(qc, kp[:, c * hd:(c + 1) * hd]) for kp in k_parts]
        m = functools.reduce(jnp.maximum, [jnp.max(s, axis=-1, keepdims=True) for s in scores])
        es = [jnp.exp2(s - m) for s in scores]
        l = functools.reduce(jnp.add, [jnp.sum(e, axis=-1, keepdims=True) for e in es])
        exps.append(es)
        weights.append(1.0 / l)
    w0 = weights[0]
    w1 = lam * weights[1]
    o = None
    for i, vp in enumerate(v_parts):
        a = (exps[0][i] * w0 - exps[1][i] * w1).astype(BF16)
        term = _dot(a, vp[...])
        o = term if o is None else o + term
    return o * _rstd(o) * subln * (1.0 - lambda_init)


def _df_body(lam_ref, subln_ref, q_ref, k_ref, v_ref, kx_ref, vx_ref, att_in_ref, o_ref, *,
             lambda_init):
    del att_in_ref
    lam = _df_lambda(lam_ref, lambda_init)
    subln = subln_ref[...]
    for sub in range(DF_Q_STEP // DF_Q_BLOCK):
        rows = slice(sub * DF_Q_BLOCK, (sub + 1) * DF_Q_BLOCK)
        o = _df_core(q_ref[rows, :], [k_ref, kx_ref], [v_ref, vx_ref], lam, subln, lambda_init)
        o_ref[rows, :] = o.astype(BF16)


def _df(qkv, att, lam4, subln4, layer, lambda_init, batch, seq, ctx_len):
    w = 2 * HEAD_DIM
    nqb = seq // DF_Q_STEP
    ctx_blk0 = batch * seq // ctx_len
    return pl.pallas_call(
        functools.partial(_df_body, lambda_init=lambda_init),
        grid=(batch, DF_HEADS, nqb),
        in_specs=[
            pl.BlockSpec((None, 4, HEAD_DIM), lambda b, h, n: (layer, 0, 0)),
            pl.BlockSpec((None, 1, w), lambda b, h, n: (layer, 0, 0)),
            pl.BlockSpec((DF_Q_STEP, w), lambda b, h, n: (b * nqb + n, OFF_QC // w + h)),
            pl.BlockSpec((seq, w), lambda b, h, n: (b, OFF_KC // w + h)),
            pl.BlockSpec((seq, w), lambda b, h, n: (b, OFF_VC // w + h)),
            pl.BlockSpec((ctx_len, w), lambda b, h, n: (ctx_blk0 + b, OFF_KC // w + h)),
            pl.BlockSpec((ctx_len, w), lambda b, h, n: (ctx_blk0 + b, OFF_VC // w + h)),
            pl.BlockSpec(memory_space=pl.ANY),
        ],
        out_specs=pl.BlockSpec((DF_Q_STEP, w), lambda b, h, n: (b * nqb + n, OFF_OC // w + h)),
        out_shape=jax.ShapeDtypeStruct(att.shape, att.dtype),
        input_output_aliases={7: 0},
        compiler_params=_params(("arbitrary", "arbitrary", "arbitrary")),
        name="df_attn",
    )(lam4, subln4, qkv, qkv, qkv, qkv, qkv, att)


def _ctx_body(sink_ref, lam_ref, subln_ref, x_ref, att_in_ref, o_ref, *, lambda_init):
    del att_in_ref
    hd = HEAD_DIM

    def cols(off, width=hd):
        return x_ref[:, off:off + width]

    for h in range(NA_HEADS):
        s = _dot_t(cols(OFF_QA + h * hd), cols(OFF_KA + h * hd))
        (p,) = _softmax_parts([s])
        o = _dot(p.astype(BF16), cols(OFF_VA + h * hd))
        o_ref[:, OFF_OA + h * hd:OFF_OA + (h + 1) * hd] = o.astype(BF16)

    for hq in range(SW_HEADS):
        kvh = hq // SW_GROUP
        s = _dot_t(cols(OFF_QB + hq * hd), cols(OFF_KB + kvh * hd))
        sink = jnp.full((s.shape[0], 1), sink_ref[hq] * LOG2E, F32)
        (p,) = _softmax_parts([s], extra=sink)
        o = _dot(p.astype(BF16), cols(OFF_VB + kvh * hd))
        o_ref[:, OFF_OB + hq * hd:OFF_OB + (hq + 1) * hd] = o.astype(BF16)

    lam = _df_lambda(lam_ref, lambda_init)
    subln = subln_ref[...]
    for h in range(DF_HEADS):
        w = 2 * hd
        o = _df_core(cols(OFF_QC + h * w, w), [cols(OFF_KC + h * w, w)], [cols(OFF_VC + h * w, w)],
                     lam, subln, lambda_init)
        o_ref[:, OFF_OC + h * w:OFF_OC + (h + 1) * w] = o.astype(BF16)


def _ctx_attn(qkv, att, sink, lam4, subln4, layer, lambda_init, batch, seq, ctx_len):
    ctx_blk0 = batch * seq // ctx_len
    return pl.pallas_call(
        functools.partial(_ctx_body, lambda_init=lambda_init),
        grid=(batch,),
        in_specs=[
            pl.BlockSpec(memory_space=pltpu.SMEM),
            pl.BlockSpec((None, 4, HEAD_DIM), lambda b: (layer, 0, 0)),
            pl.BlockSpec((None, 1, 2 * HEAD_DIM), lambda b: (layer, 0, 0)),
            pl.BlockSpec((ctx_len, IN_WIDTH), lambda b: (ctx_blk0 + b, 0)),
            pl.BlockSpec(memory_space=pl.ANY),
        ],
        out_specs=pl.BlockSpec((ctx_len, ATT_WIDTH), lambda b: (ctx_blk0 + b, 0)),
        out_shape=jax.ShapeDtypeStruct(att.shape, att.dtype),
        input_output_aliases={4: 0},
        compiler_params=_params(("arbitrary",)),
        name="ctx_attn",
    )(sink, lam4, subln4, qkv, att)


def _merge_body(h_ref, att_ref, wg_ref, bg_ref, wb_ref, o_ref, *, branch_width):
    h = h_ref[...]
    acc = None
    for i in range(3):
        gate = jax.nn.sigmoid(_dot(h, wg_ref[i]) + bg_ref[i])
        br = _dot(att_ref[:, i * branch_width:(i + 1) * branch_width], wb_ref[i].astype(BF16))
        acc = gate * br if acc is None else acc + gate * br
    o_ref[...] = acc.astype(BF16)


def _merge(h, att, w_gate, b_gate4, w_branch, layer, rows):
    d = h.shape[1]
    nb, bw = w_branch.shape[1], w_branch.shape[2]
    tm, tn = TM_MERGE, TN_MERGE
    return pl.pallas_call(
        functools.partial(_merge_body, branch_width=bw),
        grid=(rows // tm, d // tn),
        in_specs=[
            pl.BlockSpec((tm, d), lambda i, j: (i, 0)),
            pl.BlockSpec((tm, nb * bw), lambda i, j: (i, 0)),
            pl.BlockSpec((None, nb, d, tn), lambda i, j: (layer, 0, 0, j)),
            pl.BlockSpec((None, nb, 1, tn), lambda i, j: (layer, 0, 0, j)),
            pl.BlockSpec((None, nb, bw, tn), lambda i, j: (layer, 0, 0, j)),
        ],
        out_specs=pl.BlockSpec((tm, tn), lambda i, j: (i, j)),
        out_shape=jax.ShapeDtypeStruct((rows, d), BF16),
        compiler_params=_params(("arbitrary", "arbitrary")),
        name="merge",
    )(h, att, w_gate, b_gate4, w_branch)


@jax.jit
def _forward(x, c, ctx, c_ctx, w_ada, b_ada, norm_g, w_ffn_gate, w_ffn_up, w_ffn_down,
             w_in, na_rpb, sw_sink, df_lambda, df_subln_g, w_branch, w_gate, b_gate, w_out):
    batch, seq, d = x.shape
    ctx_len = ctx.shape[1]
    depth = w_ada.shape[0]
    t_lat = batch * seq
    t = t_lat + batch * ctx_len
    assert seq % TM_UP == 0 and (batch * ctx_len) % TM_UP == 0 and batch + 1 <= MOD_ROWS
    assert seq % ctx_len == 0 and ctx_len % SW_BLOCK == 0

    xs = (x.reshape(t_lat, d), ctx.reshape(batch * ctx_len, d))
    cc = jnp.zeros((MOD_ROWS, d), F32).at[:batch].set(c).at[batch].set(c_ctx)
    mod4 = _adaln(cc, w_ada, b_ada).reshape(depth, MOD_ROWS, 1, N_MOD * d)
    gains4 = norm_g.reshape(depth, norm_g.shape[1], 1, d)

    wg16 = w_ffn_gate.astype(BF16)
    wu16 = w_ffn_up.astype(BF16)
    wd16 = w_ffn_down.astype(BF16)
    wgt16 = w_gate.astype(BF16)
    wout16 = w_out.astype(BF16)
    b_gate4 = b_gate.reshape(depth, b_gate.shape[1], 1, d)
    lam4 = df_lambda.astype(F32)
    subln4 = df_subln_g.reshape(depth, 1, -1)
    cos_t, sin_t = _rope_tables(seq)

    h = _norm_mod(xs[0], xs[1], mod4, gains4, 0, seq)
    for l in range(depth):
        last = l == depth - 1
        lambda_init = 0.8 - 0.6 * math.exp(-0.3 * l)
        u = _ffn_up(h, wg16, wu16, l, 0)
        xs, h = _down_epi(u, wd16, (l, 0), xs, t, seq, mod4, gains4, l, 2, 1, 0.5,
                          nxt=(mod4, gains4, l, 2, 3, 4))
        qkv = _proj(h, w_in, l, cos_t, sin_t, seq, t_lat)
        rows = t_lat if last else t
        att = jnp.zeros((rows, ATT_WIDTH), BF16)
        att = _na(qkv, att, _na_bias_table(na_rpb[l], seq // GRID_W), batch, seq, ctx_len)
        att = _sw(qkv, att, sw_sink[l], batch, seq, ctx_len)
        att = _df(qkv, att, lam4, subln4, l, lambda_init, batch, seq, ctx_len)
        if not last:
            att = _ctx_attn(qkv, att, sw_sink[l], lam4, subln4, l, lambda_init, batch, seq, ctx_len)
        merged = _merge(h, att, wgt16, b_gate4, w_branch, l, rows)
        xs, h = _down_epi(merged, wout16, (l,), xs, rows, seq, mod4, gains4, l, 5, 3, 1.0,
                          nxt=(mod4, gains4, l, 4, 6, 7))
        u = _ffn_up(h, wg16, wu16, l, 1)
        nxt = None if last else (mod4, gains4, l + 1, 0, 0, 1)
        xs, h = _down_epi(u, wd16, (l, 1), xs, rows, seq, mod4, gains4, l, 8, 5, 0.5, nxt=nxt)
    return xs.reshape(batch, seq, d)


def kernel(x, c, ctx, c_ctx, w_ada, b_ada, norm_g, w_ffn_gate, w_ffn_up, w_ffn_down,
           w_in, na_rpb, sw_sink, df_lambda, df_subln_g, w_branch, w_gate, b_gate, w_out):
    return _forward(x, c, ctx, c_ctx, w_ada, b_ada, norm_g, w_ffn_gate, w_ffn_up, w_ffn_down,
                    w_in, na_rpb, sw_sink, df_lambda, df_subln_g, w_branch, w_gate, b_gate, w_out)
```
